```python
import jax
import jax.numpy as jnp
from jax import lax
import numpy as np

D_MODEL = 1024
BATCH = 8
SEQ = 2048
DEPTH = 2
DEC_BATCH = 128
DEC_SEQ = 1
PAST_LEN = 16384
PAGE_SIZE = 128

PLE_DIM = 256
D_FF = 4 * D_MODEL
N_BRANCH = 3
EPS = 1e-6
CHUNK = 64

S5_WIDTH = D_MODEL // 2
S5_GROUP = 16
S5_GROUPS = S5_WIDTH // S5_GROUP
S5_STATE = 64

GDN_HEADS = 4
GDN_DK = 128
GDN_DV = 128
GDN_CONV_W = 4
GDN_QK_W = GDN_HEADS * GDN_DK
GDN_V_W = GDN_HEADS * GDN_DV
GDN_CONV_CH = 2 * GDN_QK_W + GDN_V_W

ML_HEADS = 4
ML_DK = 64
ML_DV = 128
ML_QK_W = ML_HEADS * ML_DK
ML_V_W = ML_HEADS * ML_DV
GATE_CAP = 15.0

IN_SPLITS = (S5_WIDTH, GDN_CONV_CH, GDN_V_W, GDN_HEADS, GDN_HEADS,
             ML_QK_W, ML_QK_W, ML_V_W, ML_V_W, ML_HEADS, ML_HEADS, N_BRANCH * D_MODEL)
D_IN = sum(IN_SPLITS)
IN_OFFSETS = tuple(int(o) for o in np.cumsum(IN_SPLITS)[:-1])

kernel_name = 'hybrid_s5_gdn_mlstm_step'


def rmsnorm(x, g):
    xf = x.astype(jnp.float32)
    y = xf * lax.rsqrt(jnp.mean(xf * xf, axis=-1, keepdims=True) + EPS)
    return (y * g.astype(jnp.float32)).astype(x.dtype)


def head_rmsnorm(x, g):
    return x * lax.rsqrt(jnp.mean(x * x, axis=-1, keepdims=True) + EPS) * g


def l2norm(x):
    return x * lax.rsqrt(jnp.sum(x * x, axis=-1, keepdims=True) + EPS)


def softcap(x, cap):
    return cap * jnp.tanh(x / cap)


def chunk_len(L):
    return CHUNK if L % CHUNK == 0 else L


def s5_mixer(u, h0_re, h0_im, A_re, A_im, log_dt, B_re, B_im, C_re, C_im, d_skip, w_glu, b_glu):
    f = jnp.float32
    bsz, L, _ = u.shape
    uf = u.astype(f).reshape(bsz, L, S5_GROUPS, S5_GROUP)
    a_re, a_im = A_re.astype(f), A_im.astype(f)
    dt = jnp.exp(log_dt.astype(f))[:, None]
    mag = jnp.exp(a_re * dt)
    lam_re, lam_im = mag * jnp.cos(a_im * dt), mag * jnp.sin(a_im * dt)
    inv = 1.0 / (a_re * a_re + a_im * a_im)
    zr, zi = lam_re - 1.0, lam_im
    fac_re = (zr * a_re + zi * a_im) * inv
    fac_im = (zi * a_re - zr * a_im) * inv
    b_re, b_im = B_re.astype(f), B_im.astype(f)
    bb_re = fac_re[..., None] * b_re - fac_im[..., None] * b_im
    bb_im = fac_re[..., None] * b_im + fac_im[..., None] * b_re
    bu_re = jnp.einsum('blgc,gpc->blgp', uf, bb_re)
    bu_im = jnp.einsum('blgc,gpc->blgp', uf, bb_im)
    shp = (1, L, S5_GROUPS, S5_STATE)
    el_re = jnp.broadcast_to(lam_re, shp)
    el_im = jnp.broadcast_to(lam_im, shp)

    def combine(e1, e2):
        a1r, a1i, b1r, b1i = e1
        a2r, a2i, b2r, b2i = e2
        return (a2r * a1r - a2i * a1i, a2r * a1i + a2i * a1r,
                a2r * b1r - a2i * b1i + b2r, a2r * b1i + a2i * b1r + b2i)

    pr, pim, hr, hi = lax.associative_scan(combine, (el_re, el_im, bu_re, bu_im), axis=1)
    g0r = h0_re.astype(f)[:, None]
    g0i = h0_im.astype(f)[:, None]
    h_re = hr + pr * g0r - pim * g0i
    h_im = hi + pr * g0i + pim * g0r
    y = (jnp.einsum('blgp,gcp->blgc', h_re, C_re.astype(f))
         - jnp.einsum('blgp,gcp->blgc', h_im, C_im.astype(f))
         + d_skip.astype(f).reshape(S5_GROUPS, S5_GROUP) * uf)
    y = jax.nn.gelu(y.reshape(bsz, L, S5_WIDTH))
    y = y * jax.nn.sigmoid(y @ w_glu.astype(f) + b_glu.astype(f))
    dtype = u.dtype
    return y.astype(dtype), h_re[:, -1].astype(dtype), h_im[:, -1].astype(dtype)


def causal_conv(x, buf, w):
    L = x.shape[1]
    xx = jnp.concatenate([buf, x], axis=1)
    out = xx[:, 0:L] * w[0]
    for j in range(1, GDN_CONV_W):
        out = out + xx[:, j:j + L] * w[j]
    return out, xx[:, L:]


def gdn_mixer(qkv, z, b_pre, a_pre, conv_buf, S0, conv_w, A_log, dt_bias, norm_g):
    f = jnp.float32
    bsz, L, _ = qkv.shape
    conv, new_buf = causal_conv(qkv.astype(f), conv_buf.astype(f), conv_w.astype(f))
    conv = jax.nn.silu(conv)
    q, k, v = jnp.split(conv, (GDN_QK_W, 2 * GDN_QK_W), axis=-1)
    q = l2norm(q.reshape(bsz, L, GDN_HEADS, GDN_DK)) * (GDN_DK ** -0.5)
    k = l2norm(k.reshape(bsz, L, GDN_HEADS, GDN_DK))
    v = v.reshape(bsz, L, GDN_HEADS, GDN_DV)
    beta = jax.nn.sigmoid(b_pre.astype(f))
    g = -jnp.exp(A_log.astype(f)) * jax.nn.softplus(a_pre.astype(f) + dt_bias.astype(f))
    c = chunk_len(L)
    n_chunks = L // c

    def to_chunks(t):
        t = t.reshape((bsz, n_chunks, c) + t.shape[2:])
        return jnp.moveaxis(jnp.moveaxis(t, 3, 2), 1, 0)

    qc, kc, vc, gc, bc = (to_chunks(t) for t in (q, k, v, g, beta))
    gcum = jnp.cumsum(gc, axis=-1)
    idx = jnp.arange(c)
    incl = idx[:, None] >= idx[None, :]
    strict = idx[:, None] > idx[None, :]
    decay = jnp.exp(jnp.where(incl, gcum[..., :, None] - gcum[..., None, :], -jnp.inf))
    kb = kc * bc[..., None]
    a_low = jnp.where(strict, jnp.einsum('nbhtd,nbhsd->nbhts', kb, kc) * decay, 0.0)
    rhs = jnp.concatenate([vc * bc[..., None], kb * jnp.exp(gcum)[..., None]], axis=-1)
    sol = lax.linalg.triangular_solve(a_low + jnp.eye(c, dtype=f), rhs, left_side=True,
                                      lower=True, unit_diagonal=True)
    u, w = sol[..., :GDN_DV], sol[..., GDN_DV:]
    attn = jnp.where(incl, jnp.einsum('nbhtd,nbhsd->nbhts', qc, kc) * decay, 0.0)
    q_dec = qc * jnp.exp(gcum)[..., None]
    k_dec = kc * jnp.exp(gcum[..., -1:] - gcum)[..., None]
    g_last = jnp.exp(gcum[..., -1])

    def step(S, xs):
        u_i, w_i, attn_i, q_i, k_i, gl_i = xs
        v_new = u_i - jnp.einsum('bhtd,bhde->bhte', w_i, S)
        o_i = jnp.einsum('bhtd,bhde->bhte', q_i, S) + jnp.einsum('bhts,bhse->bhte', attn_i, v_new)
        S = S * gl_i[..., None, None] + jnp.einsum('bhtd,bhte->bhde', k_i, v_new)
        return S, o_i

    S_fin, o = lax.scan(step, S0.astype(f), (u, w, attn, q_dec, k_dec, g_last))
    o = jnp.moveaxis(jnp.moveaxis(o, 0, 1), 2, 3).reshape(bsz, L, GDN_HEADS, GDN_DV)
    o = head_rmsnorm(o, norm_g.astype(f)) * jax.nn.silu(z.astype(f).reshape(bsz, L, GDN_HEADS, GDN_DV))
    dtype = qkv.dtype
    return o.reshape(bsz, L, GDN_V_W).astype(dtype), new_buf.astype(dtype), S_fin.astype(dtype)


def mlstm_mixer(q, k, v, o_pre, i_pre, f_pre, C0, n0, m0, b_i, b_f, norm_g):
    f = jnp.float32
    bsz, L, _ = q.shape
    q = q.astype(f).reshape(bsz, L, ML_HEADS, ML_DK)
    k = k.astype(f).reshape(bsz, L, ML_HEADS, ML_DK) * (ML_DK ** -0.5)
    v = v.astype(f).reshape(bsz, L, ML_HEADS, ML_DV)
    li = softcap(i_pre.astype(f) + b_i.astype(f), GATE_CAP)
    lf = jax.nn.log_sigmoid(softcap(f_pre.astype(f) + b_f.astype(f), GATE_CAP))
    c = chunk_len(L)
    n_chunks = L // c

    def to_chunks(t):
        return jnp.moveaxis(t.reshape((bsz, n_chunks, c) + t.shape[2:]), 1, 0)

    qc, kc, vc, lic = (to_chunks(t) for t in (q, k, v, li))
    bcum = jnp.cumsum(to_chunks(lf), axis=2)
    idx = jnp.arange(c)
    incl = (idx[:, None] >= idx[None, :])[None, :, :, None]

    def step(carry, xs):
        C, n, m = carry
        q_i, k_i, v_i, li_i, bc_i = xs
        logw = jnp.where(incl, bc_i[:, :, None, :] - bc_i[:, None, :, :] + li_i[:, None, :, :], -jnp.inf)
        inter = bc_i + m[:, None, :]
        m_t = jnp.maximum(inter, jnp.max(logw, axis=2))
        wts = jnp.exp(logw - m_t[:, :, None, :])
        sc = jnp.exp(inter - m_t)
        s_qk = jnp.einsum('bthd,bshd->btsh', q_i, k_i) * wts
        num = jnp.einsum('btsh,bshe->bthe', s_qk, v_i) + sc[..., None] * jnp.einsum('bthd,bhde->bthe', q_i, C)
        den = jnp.sum(s_qk, axis=2) + sc * jnp.einsum('bthd,bhd->bth', q_i, n)
        h = num / jnp.maximum(jnp.abs(den), jnp.exp(-m_t))[..., None]
        w_last = wts[:, -1]
        sc_last = sc[:, -1]
        C = sc_last[..., None, None] * C + jnp.einsum('bsh,bshd,bshe->bhde', w_last, k_i, v_i)
        n = sc_last[..., None] * n + jnp.einsum('bsh,bshd->bhd', w_last, k_i)
        return (C, n, m_t[:, -1]), h

    (C_fin, n_fin, m_fin), h = lax.scan(step, (C0.astype(f), n0.astype(f), m0.astype(f)),
                                        (qc, kc, vc, lic, bcum))
    h = jnp.moveaxis(h, 0, 1).reshape(bsz, L, ML_HEADS, ML_DV)
    h = head_rmsnorm(h, norm_g.astype(f).reshape(ML_HEADS, ML_DV))
    h = h * jax.nn.sigmoid(o_pre.astype(f).reshape(bsz, L, ML_HEADS, ML_DV))
    dtype = o_pre.dtype
    return (h.reshape(bsz, L, ML_V_W).astype(dtype), C_fin.astype(dtype),
            n_fin.astype(dtype), m_fin.astype(dtype))


def block(x, p, s5_h_re, s5_h_im, conv_buf, gdn_S, ml_C, ml_n, ml_m,
          norm1_g, w_in, s5_A_re, s5_A_im, s5_log_dt, s5_B_re, s5_B_im, s5_C_re, s5_C_im,
          s5_D, s5_w_glu, s5_b_glu, gdn_conv_w, gdn_A_log, gdn_dt_bias, gdn_norm_g,
          ml_b_i, ml_b_f, ml_norm_g, w_br_s5, w_br_gdn, w_br_ml, w_out, norm2_g,
          w_up, w_down, w_ple, w_ple_gate):
    bsz, L, _ = x.shape
    xn = rmsnorm(x, norm1_g)
    (u, qkv, z, b_pre, a_pre, mq, mk, mv, mo, mi, mf, gate_pre) = jnp.split(xn @ w_in, IN_OFFSETS, axis=-1)
    y_s5, s5_h_re, s5_h_im = s5_mixer(u, s5_h_re, s5_h_im, s5_A_re, s5_A_im, s5_log_dt, s5_B_re, s5_B_im,
                                      s5_C_re, s5_C_im, s5_D, s5_w_glu, s5_b_glu)
    y_gdn, conv_buf, gdn_S = gdn_mixer(qkv, z, b_pre, a_pre, conv_buf, gdn_S, gdn_conv_w,
                                       gdn_A_log, gdn_dt_bias, gdn_norm_g)
    y_ml, ml_C, ml_n, ml_m = mlstm_mixer(mq, mk, mv, mo, mi, mf, ml_C, ml_n, ml_m, ml_b_i, ml_b_f, ml_norm_g)
    gates = jax.nn.sigmoid(gate_pre.astype(jnp.float32)).reshape(bsz, L, N_BRANCH, D_MODEL)
    merged = (gates[:, :, 0] * (y_s5 @ w_br_s5) + gates[:, :, 1] * (y_gdn @ w_br_gdn)
              + gates[:, :, 2] * (y_ml @ w_br_ml))
    x = x + (merged.astype(x.dtype) @ w_out).astype(x.dtype)
    hdn = jax.nn.relu(rmsnorm(x, norm2_g) @ w_up)
    x = x + ((hdn * hdn) @ w_down).astype(x.dtype)
    x = x + ((p @ w_ple) * jax.nn.sigmoid(x @ w_ple_gate)).astype(x.dtype)
    return x, (s5_h_re, s5_h_im, conv_buf, gdn_S, ml_C, ml_n, ml_m)


def setup_inputs(seed: int = 0) -> dict:
    key = jax.random.key(seed)
    keys = iter(jax.random.split(key, 64))
    f = jnp.float32

    def nrm(shape, scale):
        return scale * jax.random.normal(next(keys), shape, f)

    def gain(shape):
        return 1.0 + nrm(shape, 0.01)

    def log_uniform(shape, lo, hi):
        return jax.random.uniform(next(keys), shape, f, minval=float(np.log(lo)), maxval=float(np.log(hi)))

    Dp = DEPTH
    gdn_dt = jnp.exp(log_uniform((Dp, GDN_HEADS), 1e-3, 1e-1))
    return {
        'x_prompt': nrm((BATCH, SEQ, D_MODEL), 1.0),
        'x_sample': nrm((DEC_BATCH, DEC_SEQ, D_MODEL), 1.0),
        'state_s5_re': nrm((Dp, DEC_BATCH, S5_GROUPS, S5_STATE), 0.1),
        'state_s5_im': nrm((Dp, DEC_BATCH, S5_GROUPS, S5_STATE), 0.1),
        'state_gdn_conv': nrm((Dp, DEC_BATCH, GDN_CONV_W - 1, GDN_CONV_CH), 1.0),
        'state_gdn': nrm((Dp, DEC_BATCH, GDN_HEADS, GDN_DK, GDN_DV), 0.5),
        'state_mlstm_C': nrm((Dp, DEC_BATCH, ML_HEADS, ML_DK, ML_DV), 0.5),
        'state_mlstm_n': nrm((Dp, DEC_BATCH, ML_HEADS, ML_DK), 0.5),
        'state_mlstm_m': nrm((Dp, DEC_BATCH, ML_HEADS), 1.0),
        'p_prompt': nrm((Dp, BATCH, SEQ, PLE_DIM), 1.0),
        'p_sample': nrm((Dp, DEC_BATCH, DEC_SEQ, PLE_DIM), 1.0),
        'norm1_g': gain((Dp, D_MODEL)),
        'w_in': nrm((Dp, D_MODEL, D_IN), D_MODEL ** -0.5),
        's5_A_re': -0.5 + nrm((Dp, S5_GROUPS, S5_STATE), 0.01),
        's5_A_im': jnp.pi * jnp.arange(S5_STATE, dtype=f) + nrm((Dp, S5_GROUPS, S5_STATE), 0.01),
        's5_log_dt': log_uniform((Dp, S5_GROUPS), 1e-3, 1e-1),
        's5_B_re': nrm((Dp, S5_GROUPS, S5_STATE, S5_GROUP), (2 * S5_GROUP) ** -0.5),
        's5_B_im': nrm((Dp, S5_GROUPS, S5_STATE, S5_GROUP), (2 * S5_GROUP) ** -0.5),
        's5_C_re': nrm((Dp, S5_GROUPS, S5_GROUP, S5_STATE), 0.5),
        's5_C_im': nrm((Dp, S5_GROUPS, S5_GROUP, S5_STATE), 0.5),
        's5_D': nrm((Dp, S5_WIDTH), 1.0),
        's5_w_glu': nrm((Dp, S5_WIDTH, S5_WIDTH), S5_WIDTH ** -0.5),
        's5_b_glu': nrm((Dp, S5_WIDTH), 0.01),
        'gdn_conv_w': nrm((Dp, GDN_CONV_W, GDN_CONV_CH), GDN_CONV_W ** -0.5),
        'gdn_A_log': jnp.log(jax.random.uniform(next(keys), (Dp, GDN_HEADS), f, minval=1.0, maxval=16.0)),
        'gdn_dt_bias': jnp.log(jnp.expm1(gdn_dt)),
        'gdn_norm_g': gain((Dp, GDN_DV)),
        'ml_b_i': nrm((Dp, ML_HEADS), 0.1),
        'ml_b_f': 3.0 + nrm((Dp, ML_HEADS), 0.5),
        'ml_norm_g': gain((Dp, ML_V_W)),
        'w_br_s5': nrm((Dp, S5_WIDTH, D_MODEL), S5_WIDTH ** -0.5),
        'w_br_gdn': nrm((Dp, GDN_V_W, D_MODEL), GDN_V_W ** -0.5),
        'w_br_ml': nrm((Dp, ML_V_W, D_MODEL), ML_V_W ** -0.5),
        'w_out': nrm((Dp, D_MODEL, D_MODEL), D_MODEL ** -0.5),
        'norm2_g': gain((Dp, D_MODEL)),
        'w_up': nrm((Dp, D_MODEL, D_FF), D_MODEL ** -0.5),
        'w_down': nrm((Dp, D_FF, D_MODEL), D_FF ** -0.5),
        'w_ple': nrm((Dp, PLE_DIM, D_MODEL), PLE_DIM ** -0.5),
        'w_ple_gate': nrm((Dp, D_MODEL, D_MODEL), D_MODEL ** -0.5),
        'final_norm_g': gain((D_MODEL,)),
    }


def reference(x_prompt, x_sample, state_s5_re, state_s5_im, state_gdn_conv, state_gdn,
              state_mlstm_C, state_mlstm_n, state_mlstm_m, p_prompt, p_sample,
              norm1_g, w_in, s5_A_re, s5_A_im, s5_log_dt, s5_B_re, s5_B_im, s5_C_re, s5_C_im,
              s5_D, s5_w_glu, s5_b_glu, gdn_conv_w, gdn_A_log, gdn_dt_bias, gdn_norm_g,
              ml_b_i, ml_b_f, ml_norm_g, w_br_s5, w_br_gdn, w_br_ml, w_out, norm2_g,
              w_up, w_down, w_ple, w_ple_gate, final_norm_g):
    layer_weights = (norm1_g, w_in, s5_A_re, s5_A_im, s5_log_dt, s5_B_re, s5_B_im, s5_C_re, s5_C_im,
                     s5_D, s5_w_glu, s5_b_glu, gdn_conv_w, gdn_A_log, gdn_dt_bias, gdn_norm_g,
                     ml_b_i, ml_b_f, ml_norm_g, w_br_s5, w_br_gdn, w_br_ml, w_out, norm2_g,
                     w_up, w_down, w_ple, w_ple_gate)
    bp = x_prompt.shape[0]
    dt = x_prompt.dtype
    xp, xs = x_prompt, x_sample
    new_p, new_s = [], []
    for i in range(DEPTH):
        lw = [w[i] for w in layer_weights]
        zero_state = (jnp.zeros((bp, S5_GROUPS, S5_STATE), dt),
                      jnp.zeros((bp, S5_GROUPS, S5_STATE), dt),
                      jnp.zeros((bp, GDN_CONV_W - 1, GDN_CONV_CH), dt),
                      jnp.zeros((bp, GDN_HEADS, GDN_DK, GDN_DV), dt),
                      jnp.zeros((bp, ML_HEADS, ML_DK, ML_DV), dt),
                      jnp.zeros((bp, ML_HEADS, ML_DK), dt),
                      jnp.zeros((bp, ML_HEADS), dt))
        xp, sp_i = block(xp, p_prompt[i], *zero_state, *lw)
        xs, ss_i = block(xs, p_sample[i], state_s5_re[i], state_s5_im[i], state_gdn_conv[i], state_gdn[i],
                         state_mlstm_C[i], state_mlstm_n[i], state_mlstm_m[i], *lw)
        new_p.append(sp_i)
        new_s.append(ss_i)
    y_prompt = rmsnorm(xp, final_norm_g)
    y_sample = rmsnorm(xs, final_norm_g)
    sp = [jnp.stack(t) for t in zip(*new_p)]
    ss = [jnp.stack(t) for t in zip(*new_s)]
    return (y_prompt, y_sample, sp[0], ss[0], sp[1], ss[1], sp[2], ss[2], sp[3], ss[3],
            sp[4], ss[4], sp[5], ss[5], sp[6], ss[6])
```

```python
import functools
import math

import jax
import jax.numpy as jnp
from jax import lax
from jax.experimental import pallas as pl
from jax.experimental.pallas import tpu as pltpu

F32 = jnp.float32
BF16 = jnp.bfloat16
HI = lax.Precision.HIGHEST

D_MODEL = 1024
DEPTH = 2
PLE_DIM = 256
D_FF = 4 * D_MODEL
EPS = 1e-6
CHUNK = 64

S5_WIDTH = 512
S5_GROUP = 16
S5_GROUPS = 32
S5_STATE = 64
S5_NSTATE = S5_GROUPS * S5_STATE
S5_BLOCKS = 4
S5_BLOCK_STATES = S5_NSTATE // S5_BLOCKS

GDN_HEADS = 4
GDN_DK = 128
GDN_DV = 128
GDN_CONV_W = 4
GDN_CONV_CH = 1536

ML_HEADS = 4
ML_DK = 64
ML_DV = 128
ML_QK_W = ML_HEADS * ML_DK
GATE_CAP = 15.0

PROJ_W = 7168
COL_GATE = 0
COL_QKV = 3072
COL_Z = 4608
COL_MV = 5120
COL_MO = 5632
COL_MQ = 6144
COL_MK = 6400
COL_SMALL = 6656
LANE_BETA, LANE_G, LANE_I, LANE_F = 0, 4, 8, 12

VMEM_LIMIT = 52 * 1024 * 1024


def _cparams(*sem):
    return pltpu.CompilerParams(dimension_semantics=sem, vmem_limit_bytes=VMEM_LIMIT)


def _dot(a, b):
    return jnp.dot(a, b, preferred_element_type=F32)


def _dot_hi(a, b):
    return jnp.dot(a, b, preferred_element_type=F32, precision=HI)


def _dot_nt(a, b):
    return lax.dot_general(a, b, (((1,), (1,)), ((), ())), preferred_element_type=F32)


def _dot_tn(a, b):
    return lax.dot_general(a, b, (((0,), (0,)), ((), ())), preferred_element_type=F32)


def _sigmoid(x):
    return 1.0 / (1.0 + jnp.exp(-x))


def _softplus(x):
    return jnp.maximum(x, 0.0) + jnp.log(1.0 + jnp.exp(-jnp.abs(x)))


def _rms(x, g):
    return x * lax.rsqrt(jnp.mean(x * x, axis=-1, keepdims=True) + EPS) * g


def _inproj_body(x_ref, g_ref, w_ref, o_ref, xn_ref):
    @pl.when(pl.program_id(1) == 0)
    def _():
        xn_ref[...] = _rms(x_ref[...], g_ref[...]).astype(BF16)

    o_ref[...] = _dot(xn_ref[...], w_ref[...])


def _inproj(x2d, g, w, *, tm, tn, time_major=None):
    M, K = x2d.shape
    N = w.shape[1]
    grid = (M // tm, N // tn)
    if time_major is None:
        out_shape = jax.ShapeDtypeStruct((M, N), F32)
        out_spec = pl.BlockSpec((tm, tn), lambda i, j: (i, j))
    else:
        B, L = time_major
        nt = L // tm
        assert N == tn
        out_shape = jax.ShapeDtypeStruct((L, B * N), F32)
        out_spec = pl.BlockSpec((tm, tn), lambda i, j: (i % nt, i // nt))
    out = pl.pallas_call(
        _inproj_body,
        grid=grid,
        in_specs=[pl.BlockSpec((tm, K), lambda i, j: (i, 0)),
                  pl.BlockSpec((1, K), lambda i, j: (0, 0)),
                  pl.BlockSpec((K, tn), lambda i, j: (0, j))],
        out_specs=out_spec,
        out_shape=out_shape,
        scratch_shapes=[pltpu.VMEM((tm, K), BF16)],
        compiler_params=_cparams("parallel", "arbitrary"),
        name="inproj",
    )(x2d, g, w)
    if time_major is not None:
        out = out.reshape(M, N)
    return out


def _merge_body(gate_ref, ys5_ref, ygdn_ref, yml_ref, x_ref, w0_ref, w1_ref, w2_ref, wo_ref, o_ref):
    def branch(y_ref, w_ref, k):
        gate = _sigmoid(gate_ref[:, k * D_MODEL:(k + 1) * D_MODEL])
        return gate * _dot(y_ref[...].astype(BF16), w_ref[...])

    merged = branch(ys5_ref, w0_ref, 0) + branch(ygdn_ref, w1_ref, 1) + branch(yml_ref, w2_ref, 2)
    o_ref[...] = x_ref[...] + _dot(merged.astype(BF16), wo_ref[...])


def _merge(proj, ys5, ygdn, yml, x2d, w0, w1, w2, wo, *, tm, time_major=None):
    M = x2d.shape[0]
    if time_major is None:
        ys5_spec = pl.BlockSpec((tm, S5_WIDTH), lambda i: (i, 0))
    else:
        B, L = time_major
        nt = L // tm
        ys5 = ys5.reshape(L, B * S5_WIDTH)
        ys5_spec = pl.BlockSpec((tm, S5_WIDTH), lambda i: (i % nt, i // nt))
    wspec = lambda r: pl.BlockSpec((r, D_MODEL), lambda i: (0, 0))
    return pl.pallas_call(
        _merge_body,
        grid=(M // tm,),
        in_specs=[pl.BlockSpec((tm, 3 * D_MODEL), lambda i: (i, 0)),
                  ys5_spec,
                  pl.BlockSpec((tm, 512), lambda i: (i, 0)),
                  pl.BlockSpec((tm, 512), lambda i: (i, 0)),
                  pl.BlockSpec((tm, D_MODEL), lambda i: (i, 0)),
                  wspec(512), wspec(512), wspec(512), wspec(D_MODEL)],
        out_specs=pl.BlockSpec((tm, D_MODEL), lambda i: (i, 0)),
        out_shape=jax.ShapeDtypeStruct((M, D_MODEL), F32),
        compiler_params=_cparams("parallel"),
        name="merge",
    )(proj, ys5, ygdn, yml, x2d, w0, w1, w2, wo)


def _mlp_body(x_ref, g_ref, wu_ref, wd_ref, p_ref, wp_ref, wpg_ref, gf_ref, o_ref, xn_ref, acc_ref, *, final):
    j = pl.program_id(1)

    @pl.when(j == 0)
    def _():
        xn_ref[...] = _rms(x_ref[...], g_ref[...]).astype(BF16)
        acc_ref[...] = jnp.zeros_like(acc_ref)

    hdn = jnp.maximum(_dot(xn_ref[...], wu_ref[...]), 0.0)
    acc_ref[...] += _dot((hdn * hdn).astype(BF16), wd_ref[...])

    @pl.when(j == pl.num_programs(1) - 1)
    def _():
        x2 = x_ref[...] + acc_ref[...]
        ple = _dot(p_ref[...].astype(BF16), wp_ref[...]) * _sigmoid(_dot(x2.astype(BF16), wpg_ref[...]))
        x3 = x2 + ple
        o_ref[...] = _rms(x3, gf_ref[...]) if final else x3


def _mlp(x2d, g2, wu, wd, p2d, wp, wpg, gf, *, tm, tf, final):
    M = x2d.shape[0]
    return pl.pallas_call(
        functools.partial(_mlp_body, final=final),
        grid=(M // tm, D_FF // tf),
        in_specs=[pl.BlockSpec((tm, D_MODEL), lambda i, j: (i, 0)),
                  pl.BlockSpec((1, D_MODEL), lambda i, j: (0, 0)),
                  pl.BlockSpec((D_MODEL, tf), lambda i, j: (0, j)),
                  pl.BlockSpec((tf, D_MODEL), lambda i, j: (j, 0)),
                  pl.BlockSpec((tm, PLE_DIM), lambda i, j: (i, 0)),
                  pl.BlockSpec((PLE_DIM, D_MODEL), lambda i, j: (0, 0)),
                  pl.BlockSpec((D_MODEL, D_MODEL), lambda i, j: (0, 0)),
                  pl.BlockSpec((1, D_MODEL), lambda i, j: (0, 0))],
        out_specs=pl.BlockSpec((tm, D_MODEL), lambda i, j: (i, 0)),
        out_shape=jax.ShapeDtypeStruct((M, D_MODEL), F32),
        scratch_shapes=[pltpu.VMEM((tm, D_MODEL), BF16), pltpu.VMEM((tm, D_MODEL), F32)],
        compiler_params=_cparams("parallel", "arbitrary"),
        name="mlp",
    )(x2d, g2, wu, wd, p2d, wp, wpg, gf)


def _s5_body(u_ref, h0r_ref, h0i_ref, lr_ref, li_ref, bbr_ref, bbi_ref, cbr_ref, cbi_ref, d_ref, wg_ref, bg_ref,
             y_ref, hfr_ref, hfi_ref, sr_ref, si_ref, bur_ref, bui_ref, *, rows, steps):
    i = pl.program_id(0)

    @pl.when(i == 0)
    def _():
        sr_ref[...] = h0r_ref[...]
        si_ref[...] = h0i_ref[...]

    u = u_ref[...]
    ub = u.astype(BF16)
    nb = S5_BLOCK_STATES
    for blk in range(S5_BLOCKS):
        ublk = ub[:, blk * 128:(blk + 1) * 128]
        bur_ref[:, blk * nb:(blk + 1) * nb] = _dot(ublk, bbr_ref[blk])
        bui_ref[:, blk * nb:(blk + 1) * nb] = _dot(ublk, bbi_ref[blk])

    for blk in range(S5_BLOCKS):
        cols = slice(blk * nb, (blk + 1) * nb)
        lr = jnp.broadcast_to(lr_ref[:, cols], (rows, nb))
        li = jnp.broadcast_to(li_ref[:, cols], (rows, nb))

        def step(t, carry, cols=cols, lr=lr, li=li):
            hr, hi = carry
            r = pl.ds(pl.multiple_of(t * rows, rows), rows)
            nr = lr * hr - li * hi + bur_ref[r, cols]
            ni = lr * hi + li * hr + bui_ref[r, cols]
            bur_ref[r, cols] = nr
            bui_ref[r, cols] = ni
            return nr, ni

        carry = (sr_ref[:, cols], si_ref[:, cols])
        if steps == 1:
            carry = step(0, carry)
        else:
            carry = lax.fori_loop(0, steps, step, carry, unroll=8)
        sr_ref[:, cols] = carry[0]
        si_ref[:, cols] = carry[1]

    ys = []
    for blk in range(S5_BLOCKS):
        cols = slice(blk * nb, (blk + 1) * nb)
        ys.append(_dot(bur_ref[:, cols].astype(BF16), cbr_ref[blk])
                  - _dot(bui_ref[:, cols].astype(BF16), cbi_ref[blk]))
    y = jnp.concatenate(ys, axis=1) + d_ref[...] * u
    y = 0.5 * y * (1.0 + jnp.tanh(math.sqrt(2.0 / math.pi) * (y + 0.044715 * (y * y * y))))
    y_ref[...] = y * _sigmoid(_dot(y.astype(BF16), wg_ref[...]) + bg_ref[...])

    @pl.when(i == pl.num_programs(0) - 1)
    def _():
        hfr_ref[...] = sr_ref[...]
        hfi_ref[...] = si_ref[...]


def _s5(u_tm, h0r, h0i, prm, *, rows, steps):
    n = u_tm.shape[0]
    tb = rows * steps
    full = lambda shape: pl.BlockSpec(shape, lambda i: (0,) * len(shape))
    return pl.pallas_call(
        functools.partial(_s5_body, rows=rows, steps=steps),
        grid=(n // tb,),
        in_specs=[pl.BlockSpec((tb, S5_WIDTH), lambda i: (i, 0)),
                  full((rows, S5_NSTATE)), full((rows, S5_NSTATE)),
                  full((1, S5_NSTATE)), full((1, S5_NSTATE)),
                  full((S5_BLOCKS, 128, S5_BLOCK_STATES)), full((S5_BLOCKS, 128, S5_BLOCK_STATES)),
                  full((S5_BLOCKS, S5_BLOCK_STATES, 128)), full((S5_BLOCKS, S5_BLOCK_STATES, 128)),
                  full((1, S5_WIDTH)), full((S5_WIDTH, S5_WIDTH)), full((1, S5_WIDTH))],
        out_specs=[pl.BlockSpec((tb, S5_WIDTH), lambda i: (i, 0)),
                   full((rows, S5_NSTATE)), full((rows, S5_NSTATE))],
        out_shape=[jax.ShapeDtypeStruct((n, S5_WIDTH), F32),
                   jax.ShapeDtypeStruct((rows, S5_NSTATE), F32),
                   jax.ShapeDtypeStruct((rows, S5_NSTATE), F32)],
        scratch_shapes=[pltpu.VMEM((rows, S5_NSTATE), F32), pltpu.VMEM((rows, S5_NSTATE), F32),
                        pltpu.VMEM((tb, S5_NSTATE), F32), pltpu.VMEM((tb, S5_NSTATE), F32)],
        compiler_params=_cparams("arbitrary"),
        name="s5",
    )(u_tm, h0r, h0i, prm["lam_re"], prm["lam_im"], prm["bbd_re"], prm["bbd_im"],
      prm["cbd_re"], prm["cbd_im"], prm["d"], prm["w_glu"], prm["b_glu"])


def _s5_params(A_re, A_im, log_dt, B_re, B_im, C_re, C_im, d_skip, w_glu, b_glu):
    dt = jnp.exp(log_dt)[:, None]
    mag = jnp.exp(A_re * dt)
    lam_re, lam_im = mag * jnp.cos(A_im * dt), mag * jnp.sin(A_im * dt)
    inv = 1.0 / (A_re * A_re + A_im * A_im)
    zr, zi = lam_re - 1.0, lam_im
    fac_re = (zr * A_re + zi * A_im) * inv
    fac_im = (zi * A_re - zr * A_im) * inv
    bb_re = fac_re[..., None] * B_re - fac_im[..., None] * B_im
    bb_im = fac_re[..., None] * B_im + fac_im[..., None] * B_re
    gpb = S5_GROUPS // S5_BLOCKS
    eye = jnp.eye(gpb, dtype=F32)

    def in_map(bb):
        t = bb.reshape(S5_BLOCKS, gpb, S5_STATE, S5_GROUP).transpose(0, 1, 3, 2)
        return jnp.einsum('bgcp,gh->bgchp', t, eye).reshape(S5_BLOCKS, 128, S5_BLOCK_STATES).astype(BF16)

    def out_map(cc):
        t = cc.reshape(S5_BLOCKS, gpb, S5_GROUP, S5_STATE).transpose(0, 1, 3, 2)
        return jnp.einsum('bgpc,gh->bgphc', t, eye).reshape(S5_BLOCKS, S5_BLOCK_STATES, 128).astype(BF16)

    return dict(lam_re=lam_re.reshape(1, S5_NSTATE), lam_im=lam_im.reshape(1, S5_NSTATE),
                bbd_re=in_map(bb_re), bbd_im=in_map(bb_im), cbd_re=out_map(C_re), cbd_im=out_map(C_im),
                d=d_skip.reshape(1, S5_WIDTH), w_glu=w_glu.astype(BF16), b_glu=b_glu.reshape(1, S5_WIDTH))


def _tri_masks(c):
    row = lax.broadcasted_iota(jnp.int32, (c, c), 0)
    col = lax.broadcasted_iota(jnp.int32, (c, c), 1)
    return row >= col, row > col


def _row_valid(c, valid):
    return lax.broadcasted_iota(jnp.int32, (c, 1), 0) < valid


def _gdn_body(qkv_ref, z_ref, sm_ref, cb0_ref, s0_ref, cw_ref, sb_ref, nega_ref, ng_ref,
              o_ref, cbo_ref, so_ref, xs_ref, s_ref, *, c, nc, valid):
    j = pl.program_id(1)
    tb = c * nc

    @pl.when(j == 0)
    def _():
        xs_ref[5:8, :] = cb0_ref[...]
        s_ref[...] = s0_ref[...]

    if valid < tb:
        xs_ref[8:8 + tb, :] = jnp.zeros((tb, GDN_CONV_CH), F32)
    xs_ref[8:8 + valid, :] = qkv_ref[...]

    incl, strict = _tri_masks(c)
    tri = incl.astype(F32)
    cw = cw_ref[...]
    nsteps = max(1, int(math.ceil(math.log2(c))))

    for ci in range(nc):
        r0 = 8 + ci * c
        conv = xs_ref[r0 - 3:r0 - 3 + c, :] * cw[0:1]
        conv = conv + xs_ref[r0 - 2:r0 - 2 + c, :] * cw[1:2]
        conv = conv + xs_ref[r0 - 1:r0 - 1 + c, :] * cw[2:3]
        conv = conv + xs_ref[r0:r0 + c, :] * cw[3:4]
        conv = conv * _sigmoid(conv)
        rows = slice(ci * c, (ci + 1) * c)
        if valid < tb:
            rv = _row_valid(c, valid)
            sm = jnp.where(rv, jnp.broadcast_to(sm_ref[0:1, :], (c, 128)), 0.0)
            zz = jnp.where(rv, jnp.broadcast_to(z_ref[0:1, :], (c, 512)), 0.0)
            rvf = rv.astype(F32)
            conv = conv * rvf
        else:
            sm = sm_ref[rows, :]
            zz = z_ref[rows, :]
            rvf = None
        pre = sm + sb_ref[...]
        beta_all = _sigmoid(pre)
        g_all = nega_ref[...] * _softplus(pre)
        if rvf is not None:
            beta_all = beta_all * rvf
            g_all = g_all * rvf
        gcum = _dot_hi(tri, g_all)
        gcum_t = gcum.T
        for h in range(GDN_HEADS):
            q = conv[:, 128 * h:128 * (h + 1)]
            k = conv[:, 512 + 128 * h:512 + 128 * (h + 1)]
            v = conv[:, 1024 + 128 * h:1024 + 128 * (h + 1)]
            q = q * lax.rsqrt(jnp.sum(q * q, axis=-1, keepdims=True) + EPS) * (GDN_DK ** -0.5)
            k = k * lax.rsqrt(jnp.sum(k * k, axis=-1, keepdims=True) + EPS)
            beta = beta_all[:, LANE_BETA + h:LANE_BETA + h + 1]
            gc = gcum[:, LANE_G + h:LANE_G + h + 1]
            gr = gcum_t[LANE_G + h:LANE_G + h + 1, :]
            decay = jnp.exp(jnp.where(incl, gc - gr, -jnp.inf))
            kb = k * beta
            a_low = jnp.where(strict, _dot_nt(kb, k) * decay, 0.0)
            eg = jnp.exp(gc)
            x = jnp.concatenate([v * beta, kb * eg], axis=1)
            m = -a_low
            for it in range(nsteps):
                x = x + _dot_hi(m, x)
                if it < nsteps - 1:
                    m = _dot_hi(m, m)
            uu, ww = x[:, :GDN_DV], x[:, GDN_DV:]
            attn = jnp.where(incl, _dot_nt(q, k) * decay, 0.0)
            g_end = gc[c - 1:c, :]
            q_dec = q * eg
            k_dec = k * jnp.exp(g_end - gc)
            s_old = s_ref[h]
            v_new = uu - _dot(ww, s_old)
            o = _dot(q_dec, s_old) + _dot(attn, v_new)
            s_ref[h] = s_old * jnp.exp(g_end) + _dot_tn(k_dec, v_new)
            o = o * lax.rsqrt(jnp.mean(o * o, axis=-1, keepdims=True) + EPS) * ng_ref[...]
            zh = zz[:, 128 * h:128 * (h + 1)]
            o = o * (zh * _sigmoid(zh))
            lo = ci * c
            nv_out = min(max(valid - lo, 0), c)
            if nv_out > 0:
                o_ref[lo:lo + nv_out, 128 * h:128 * (h + 1)] = o[:nv_out]

    tail = xs_ref[8 + valid - 3:8 + valid, :]
    xs_ref[5:8, :] = tail
    cbo_ref[...] = tail

    @pl.when(j == pl.num_programs(1) - 1)
    def _():
        so_ref[...] = s_ref[...]


def _gdn(proj3, cb0, s0, prm, *, c, nc, valid):
    B, Lr, _ = proj3.shape
    nblk = Lr // valid
    cq, cz, cs = COL_QKV // 1536, COL_Z // 512, COL_SMALL // 128
    return pl.pallas_call(
        functools.partial(_gdn_body, c=c, nc=nc, valid=valid),
        grid=(B, nblk),
        in_specs=[pl.BlockSpec((None, valid, 1536), lambda b, j: (b, j, cq)),
                  pl.BlockSpec((None, valid, 512), lambda b, j: (b, j, cz)),
                  pl.BlockSpec((None, valid, 128), lambda b, j: (b, j, cs)),
                  pl.BlockSpec((None, 3, GDN_CONV_CH), lambda b, j: (b, 0, 0)),
                  pl.BlockSpec((None, GDN_HEADS, GDN_DK, GDN_DV), lambda b, j: (b, 0, 0, 0)),
                  pl.BlockSpec((GDN_CONV_W, GDN_CONV_CH), lambda b, j: (0, 0)),
                  pl.BlockSpec((1, 128), lambda b, j: (0, 0)),
                  pl.BlockSpec((1, 128), lambda b, j: (0, 0)),
                  pl.BlockSpec((1, 128), lambda b, j: (0, 0))],
        out_specs=[pl.BlockSpec((None, valid, 512), lambda b, j: (b, j, 0)),
                   pl.BlockSpec((None, 3, GDN_CONV_CH), lambda b, j: (b, 0, 0)),
                   pl.BlockSpec((None, GDN_HEADS, GDN_DK, GDN_DV), lambda b, j: (b, 0, 0, 0))],
        out_shape=[jax.ShapeDtypeStruct((B, Lr, 512), F32),
                   jax.ShapeDtypeStruct((B, 3, GDN_CONV_CH), F32),
                   jax.ShapeDtypeStruct((B, GDN_HEADS, GDN_DK, GDN_DV), F32)],
        scratch_shapes=[pltpu.VMEM((8 + c * nc, GDN_CONV_CH), F32),
                        pltpu.VMEM((GDN_HEADS, GDN_DK, GDN_DV), F32)],
        compiler_params=_cparams("parallel", "arbitrary"),
        name="gdn",
    )(proj3, proj3, proj3, cb0, s0, prm["conv_w"], prm["bias"], prm["nega"], prm["norm_g"])


def _mlstm_body(q_ref, k_ref, v_ref, og_ref, sm_ref, c0_ref, n0_ref, m0_ref, sb_ref, ng_ref,
                o_ref, co_ref, no_ref, mo_ref, c_ref, n_ref, m_ref, *, c, nc, valid):
    j = pl.program_id(1)
    tb = c * nc

    @pl.when(j == 0)
    def _():
        c_ref[...] = c0_ref[...]
        n_ref[...] = n0_ref[...]
        m_ref[...] = m0_ref[...]

    incl, _ = _tri_masks(c)
    tri = incl.astype(F32)
    lane = lax.broadcasted_iota(jnp.int32, (1, ML_QK_W), 1)
    srow = lax.broadcasted_iota(jnp.int32, (ML_QK_W, 1), 0)
    cap = GATE_CAP

    for ci in range(nc):
        rows = slice(ci * c, (ci + 1) * c)
        if valid < tb:
            rv = _row_valid(c, valid)

            def pad(ref, w):
                return jnp.where(rv, jnp.broadcast_to(ref[0:1, :], (c, w)), 0.0)

            sm, q, k, v, og = pad(sm_ref, 128), pad(q_ref, 256), pad(k_ref, 256), pad(v_ref, 512), pad(og_ref, 512)
        else:
            rv = None
            sm, q, k, v, og = sm_ref[rows, :], q_ref[rows, :], k_ref[rows, :], v_ref[rows, :], og_ref[rows, :]
        k = k * (ML_DK ** -0.5)
        pre = sm + sb_ref[...]
        capped = cap * jnp.tanh(pre / cap)
        li_all = capped
        lf_all = -_softplus(-capped)
        if rv is not None:
            li_all = jnp.where(rv, li_all, -jnp.inf)
            lf_all = jnp.where(rv, lf_all, 0.0)
        bcum = _dot_hi(tri, lf_all)
        bcum_t = bcum.T
        li_t = li_all.T
        c_old = c_ref[...]
        n_old = n_ref[...]
        c_new = jnp.zeros((ML_QK_W, ML_DV), F32)
        n_new = jnp.zeros((1, ML_QK_W), F32)
        sc_rows = jnp.zeros((ML_QK_W, 1), F32)
        sc_lanes = jnp.zeros((1, ML_QK_W), F32)
        for h in range(ML_HEADS):
            hmask = ((lane >= ML_DK * h) & (lane < ML_DK * (h + 1))).astype(F32)
            qh = q * hmask
            bc = bcum[:, LANE_F + h:LANE_F + h + 1]
            br = bcum_t[LANE_F + h:LANE_F + h + 1, :]
            lir = li_t[LANE_I + h:LANE_I + h + 1, :]
            lic = li_all[:, LANE_I + h:LANE_I + h + 1]
            logw = jnp.where(incl, bc - br + lir, -jnp.inf)
            m_prev = m_ref[h:h + 1, 0:1]
            inter = bc + m_prev
            m_t = jnp.maximum(inter, jnp.max(logw, axis=1, keepdims=True))
            wts = jnp.exp(logw - m_t)
            sc = jnp.exp(inter - m_t)
            s_qk = _dot_nt(qh, k) * wts
            vh = v[:, ML_DV * h:ML_DV * (h + 1)]
            num = _dot(s_qk, vh) + sc * _dot(qh, c_old)
            den = jnp.sum(s_qk, axis=1, keepdims=True) + sc * jnp.sum(qh * n_old, axis=1, keepdims=True)
            hh = num / jnp.maximum(jnp.abs(den), jnp.exp(-m_t))
            m_end = m_t[c - 1:c, :]
            w_last = jnp.exp(bc[c - 1:c, :] - bc + lic - m_end)
            sc_end = sc[c - 1:c, :]
            kw = k * hmask * w_last
            c_new = c_new + _dot_tn(kw, vh)
            n_new = n_new + jnp.sum(kw, axis=0, keepdims=True)
            sc_rows = sc_rows + sc_end * ((srow >= ML_DK * h) & (srow < ML_DK * (h + 1))).astype(F32)
            sc_lanes = sc_lanes + sc_end * hmask
            m_ref[h:h + 1, :] = jnp.broadcast_to(m_end, (1, 128))
            hh = hh * lax.rsqrt(jnp.mean(hh * hh, axis=-1, keepdims=True) + EPS) * ng_ref[:, ML_DV * h:ML_DV * (h + 1)]
            hh = hh * _sigmoid(og[:, ML_DV * h:ML_DV * (h + 1)])
            lo = ci * c
            nv_out = min(max(valid - lo, 0), c)
            if nv_out > 0:
                o_ref[lo:lo + nv_out, ML_DV * h:ML_DV * (h + 1)] = hh[:nv_out]
        c_ref[...] = sc_rows * c_old + c_new
        n_ref[...] = sc_lanes * n_old + n_new

    @pl.when(j == pl.num_programs(1) - 1)
    def _():
        co_ref[...] = c_ref[...]
        no_ref[...] = n_ref[...]
        mo_ref[...] = m_ref[...]


def _mlstm(proj3, c0, n0, m0, prm, *, c, nc, valid):
    B, Lr, _ = proj3.shape
    nblk = Lr // valid
    cq, ck, cv, co, cs = COL_MQ // 256, COL_MK // 256, COL_MV // 512, COL_MO // 512, COL_SMALL // 128
    st = lambda shape: pl.BlockSpec((None,) + shape, lambda b, j: (b,) + (0,) * len(shape))
    return pl.pallas_call(
        functools.partial(_mlstm_body, c=c, nc=nc, valid=valid),
        grid=(B, nblk),
        in_specs=[pl.BlockSpec((None, valid, 256), lambda b, j: (b, j, cq)),
                  pl.BlockSpec((None, valid, 256), lambda b, j: (b, j, ck)),
                  pl.BlockSpec((None, valid, 512), lambda b, j: (b, j, cv)),
                  pl.BlockSpec((None, valid, 512), lambda b, j: (b, j, co)),
                  pl.BlockSpec((None, valid, 128), lambda b, j: (b, j, cs)),
                  st((ML_QK_W, ML_DV)), st((1, ML_QK_W)), st((8, 128)),
                  pl.BlockSpec((1, 128), lambda b, j: (0, 0)),
                  pl.BlockSpec((1, 512), lambda b, j: (0, 0))],
        out_specs=[pl.BlockSpec((None, valid, 512), lambda b, j: (b, j, 0)),
                   st((ML_QK_W, ML_DV)), st((1, ML_QK_W)), st((8, 128))],
        out_shape=[jax.ShapeDtypeStruct((B, Lr, 512), F32),
                   jax.ShapeDtypeStruct((B, ML_QK_W, ML_DV), F32),
                   jax.ShapeDtypeStruct((B, 1, ML_QK_W), F32),
                   jax.ShapeDtypeStruct((B, 8, 128), F32)],
        scratch_shapes=[pltpu.VMEM((ML_QK_W, ML_DV), F32), pltpu.VMEM((1, ML_QK_W), F32),
                        pltpu.VMEM((8, 128), F32)],
        compiler_params=_cparams("parallel", "arbitrary"),
        name="mlstm",
    )(proj3, proj3, proj3, proj3, proj3, c0, n0, m0, prm["bias"], prm["norm_g"])


def _lane_vec(pieces):
    v = jnp.zeros((128,), F32)
    for off, val in pieces:
        v = v.at[off:off + val.shape[0]].set(val.astype(F32))
    return v.reshape(1, 128)


def _layer_params(i, w):
    win = w["w_in"][i]
    o = [0, 512, 2048, 2560, 2564, 2568, 2824, 3080, 3592, 4104, 4108, 4112, 7184]
    seg = lambda a: win[:, o[a]:o[a + 1]]
    w_u, w_qkv, w_z, w_b, w_a, w_mq, w_mk, w_mv, w_mo, w_mi, w_mf, w_gate = (seg(a) for a in range(12))
    small = jnp.concatenate([w_b, w_a, w_mi, w_mf, jnp.zeros((D_MODEL, 512 - 16), F32)], axis=1)
    w_main = jnp.concatenate([w_gate, w_qkv, w_z, w_mv, w_mo, w_mq, w_mk, small], axis=1).astype(BF16)
    bias = _lane_vec([(LANE_G, w["gdn_dt_bias"][i]), (LANE_I, w["ml_b_i"][i]), (LANE_F, w["ml_b_f"][i])])
    return dict(
        norm1_g=w["norm1_g"][i].reshape(1, D_MODEL), w_u=w_u.astype(BF16), w_main=w_main,
        s5=_s5_params(w["s5_A_re"][i], w["s5_A_im"][i], w["s5_log_dt"][i], w["s5_B_re"][i], w["s5_B_im"][i],
                      w["s5_C_re"][i], w["s5_C_im"][i], w["s5_D"][i], w["s5_w_glu"][i], w["s5_b_glu"][i]),
        gdn=dict(conv_w=w["gdn_conv_w"][i], bias=bias, nega=_lane_vec([(LANE_G, -jnp.exp(w["gdn_A_log"][i]))]),
                 norm_g=w["gdn_norm_g"][i].reshape(1, GDN_DV)),
        ml=dict(bias=bias, norm_g=w["ml_norm_g"][i].reshape(1, 512)),
        w_br_s5=w["w_br_s5"][i].astype(BF16), w_br_gdn=w["w_br_gdn"][i].astype(BF16),
        w_br_ml=w["w_br_ml"][i].astype(BF16), w_out=w["w_out"][i].astype(BF16),
        norm2_g=w["norm2_g"][i].reshape(1, D_MODEL), w_up=w["w_up"][i].astype(BF16),
        w_down=w["w_down"][i].astype(BF16), w_ple=w["w_ple"][i].astype(BF16),
        w_ple_gate=w["w_ple_gate"][i].astype(BF16))


def _block(x2d, p2d, B, L, state, lp, gf, final):
    s5r, s5i, cbuf, gS, mC, mn, mm = state
    prompt = L > 1
    if prompt:
        tm, tmm, tml, tf = 1024, 256, 1024, 1024
        tmaj = (B, L)
        s5_rows, s5_steps = B, CHUNK
        c, nc, valid = CHUNK, 1, CHUNK
    else:
        tm = tmm = tml = B
        tf = 1024
        tmaj = None
        s5_rows, s5_steps = B, 1
        c, nc, valid = 8, 1, 1
    u_tm = _inproj(x2d, lp["norm1_g"], lp["w_u"], tm=tm, tn=S5_WIDTH, time_major=tmaj)
    proj = _inproj(x2d, lp["norm1_g"], lp["w_main"], tm=tm, tn=1024)
    y_s5, h_re, h_im = _s5(u_tm, s5r.reshape(B, S5_NSTATE), s5i.reshape(B, S5_NSTATE), lp["s5"],
                           rows=s5_rows, steps=s5_steps)
    proj3 = proj.reshape(B, L, PROJ_W)
    y_gdn, cbuf_n, gS_n = _gdn(proj3, cbuf, gS, lp["gdn"], c=c, nc=nc, valid=valid)
    m_in = jnp.broadcast_to(jnp.pad(mm, ((0, 0), (0, 8 - ML_HEADS)))[:, :, None], (B, 8, 128))
    y_ml, mC_n, mn_n, mm_n = _mlstm(proj3, mC.reshape(B, ML_QK_W, ML_DV), mn.reshape(B, 1, ML_QK_W), m_in,
                                    lp["ml"], c=c, nc=nc, valid=valid)
    x2d = _merge(proj, y_s5, y_gdn.reshape(B * L, 512), y_ml.reshape(B * L, 512), x2d,
                 lp["w_br_s5"], lp["w_br_gdn"], lp["w_br_ml"], lp["w_out"], tm=tmm, time_major=tmaj)
    x2d = _mlp(x2d, lp["norm2_g"], lp["w_up"], lp["w_down"], p2d, lp["w_ple"], lp["w_ple_gate"], gf,
               tm=tml, tf=tf, final=final)
    new_state = (h_re.reshape(B, S5_GROUPS, S5_STATE), h_im.reshape(B, S5_GROUPS, S5_STATE), cbuf_n, gS_n,
                 mC_n.reshape(B, ML_HEADS, ML_DK, ML_DV), mn_n.reshape(B, ML_HEADS, ML_DK), mm_n[:, :ML_HEADS, 0])
    return x2d, new_state


def kernel(x_prompt, x_sample, state_s5_re, state_s5_im, state_gdn_conv, state_gdn, state_mlstm_C, state_mlstm_n, state_mlstm_m, p_prompt, p_sample, norm1_g, w_in, s5_A_re, s5_A_im, s5_log_dt, s5_B_re, s5_B_im, s5_C_re, s5_C_im, s5_D, s5_w_glu, s5_b_glu, gdn_conv_w, gdn_A_log, gdn_dt_bias, gdn_norm_g, ml_b_i, ml_b_f, ml_norm_g, w_br_s5, w_br_gdn, w_br_ml, w_out, norm2_g, w_up, w_down, w_ple, w_ple_gate, final_norm_g):
    w = dict(norm1_g=norm1_g, w_in=w_in, s5_A_re=s5_A_re, s5_A_im=s5_A_im, s5_log_dt=s5_log_dt, s5_B_re=s5_B_re,
             s5_B_im=s5_B_im, s5_C_re=s5_C_re, s5_C_im=s5_C_im, s5_D=s5_D, s5_w_glu=s5_w_glu, s5_b_glu=s5_b_glu,
             gdn_conv_w=gdn_conv_w, gdn_A_log=gdn_A_log, gdn_dt_bias=gdn_dt_bias, gdn_norm_g=gdn_norm_g,
             ml_b_i=ml_b_i, ml_b_f=ml_b_f, ml_norm_g=ml_norm_g, w_br_s5=w_br_s5, w_br_gdn=w_br_gdn,
             w_br_ml=w_br_ml, w_out=w_out, norm2_g=norm2_g, w_up=w_up, w_down=w_down, w_ple=w_ple,
             w_ple_gate=w_ple_gate)
    bp, lp_len, _ = x_prompt.shape
    bs, ls_len, _ = x_sample.shape
    gf = final_norm_g.reshape(1, D_MODEL)
    xp = x_prompt.reshape(bp * lp_len, D_MODEL)
    xs = x_sample.reshape(bs * ls_len, D_MODEL)
    new_p, new_s = [], []
    for i in range(DEPTH):
        lp = _layer_params(i, w)
        zero_state = (jnp.zeros((bp, S5_GROUPS, S5_STATE), F32), jnp.zeros((bp, S5_GROUPS, S5_STATE), F32),
                      jnp.zeros((bp, GDN_CONV_W - 1, GDN_CONV_CH), F32),
                      jnp.zeros((bp, GDN_HEADS, GDN_DK, GDN_DV), F32),
                      jnp.zeros((bp, ML_HEADS, ML_DK, ML_DV), F32), jnp.zeros((bp, ML_HEADS, ML_DK), F32),
                      jnp.zeros((bp, ML_HEADS), F32))
        final = i == DEPTH - 1
        xp, sp_i = _block(xp, p_prompt[i].reshape(bp * lp_len, PLE_DIM), bp, lp_len, zero_state, lp, gf, final)
        sample_state = (state_s5_re[i], state_s5_im[i], state_gdn_conv[i], state_gdn[i], state_mlstm_C[i],
                        state_mlstm_n[i], state_mlstm_m[i])
        xs, ss_i = _block(xs, p_sample[i].reshape(bs * ls_len, PLE_DIM), bs, ls_len, sample_state, lp, gf, final)
        new_p.append(sp_i)
        new_s.append(ss_i)
    sp = [jnp.stack(t) for t in zip(*new_p)]
    ss = [jnp.stack(t) for t in zip(*new_s)]
    return (xp.reshape(bp, lp_len, D_MODEL), xs.reshape(bs, ls_len, D_MODEL),
            sp[0], ss[0], sp[1], ss[1], sp[2], ss[2], sp[3], ss[3], sp[4], ss[4], sp[5], ss[5], sp[6], ss[6])
```

```python
import functools
import math

import jax
import jax.numpy as jnp
from jax import lax
from jax.experimental import pallas as pl
from jax.experimental.pallas import tpu as pltpu

F32 = jnp.float32
BF16 = jnp.bfloat16
HI = lax.Precision.HIGHEST

D_MODEL = 1024
DEPTH = 2
PLE_DIM = 256
D_FF = 4 * D_MODEL
EPS = 1e-6
CHUNK = 64

S5_WIDTH = 512
S5_GROUP = 16
S5_GROUPS = 32
S5_STATE = 64
S5_NSTATE = S5_GROUPS * S5_STATE
S5_BLOCKS = 4
S5_BLOCK_STATES = S5_NSTATE // S5_BLOCKS

GDN_HEADS = 4
GDN_DK = 128
GDN_DV = 128
GDN_CONV_W = 4
GDN_CONV_CH = 1536

ML_HEADS = 4
ML_DK = 64
ML_DV = 128
ML_QK_W = ML_HEADS * ML_DK
GATE_CAP = 15.0

PROJ_W = 7296
PROJ_TN = PROJ_W // 3
COL_GATE = 0
COL_QKV = 3072
COL_Z = 4608
COL_MV = 5120
COL_MO = 5632
COL_U = 6144
COL_MQ = 6656
COL_MK = 6912
COL_SMALL = 7168
LANE_BETA, LANE_G, LANE_I, LANE_F = 0, 4, 8, 12

STEP_ROWS = 8

VMEM_LIMIT = 52 * 1024 * 1024


def _cparams(*sem):
    return pltpu.CompilerParams(dimension_semantics=sem, vmem_limit_bytes=VMEM_LIMIT)


def _dot(a, b):
    return jnp.dot(a, b, preferred_element_type=F32)


def _dot_hi(a, b):
    return jnp.dot(a, b, preferred_element_type=F32, precision=HI)


def _dot_nt(a, b):
    return lax.dot_general(a, b, (((1,), (1,)), ((), ())), preferred_element_type=F32)


def _dot_tn(a, b):
    return lax.dot_general(a, b, (((0,), (0,)), ((), ())), preferred_element_type=F32)


def _sigmoid(x):
    return 1.0 / (1.0 + jnp.exp(-x))


def _softplus(x):
    return jnp.maximum(x, 0.0) + jnp.log(1.0 + jnp.exp(-jnp.abs(x)))


def _rms(x, g):
    return x * lax.rsqrt(jnp.mean(x * x, axis=-1, keepdims=True) + EPS) * g


def _inproj_body(x_ref, g_ref, w_ref, o_ref, xn_ref):
    @pl.when(pl.program_id(1) == 0)
    def _():
        xn_ref[...] = _rms(x_ref[...], g_ref[...]).astype(BF16)

    o_ref[...] = _dot(xn_ref[...], w_ref[...])


def _inproj(x2d, g, w, *, tm, tn):
    M, K = x2d.shape
    N = w.shape[1]
    return pl.pallas_call(
        _inproj_body,
        grid=(M // tm, N // tn),
        in_specs=[pl.BlockSpec((tm, K), lambda i, j: (i, 0)),
                  pl.BlockSpec((1, K), lambda i, j: (0, 0)),
                  pl.BlockSpec((K, tn), lambda i, j: (0, j))],
        out_specs=pl.BlockSpec((tm, tn), lambda i, j: (i, j)),
        out_shape=jax.ShapeDtypeStruct((M, N), F32),
        scratch_shapes=[pltpu.VMEM((tm, K), BF16)],
        compiler_params=_cparams("parallel", "arbitrary"),
        name="inproj",
    )(x2d, g, w)


def _merge_body(gate_ref, ys5_ref, ygdn_ref, yml_ref, x_ref, w0_ref, w1_ref, w2_ref, wo_ref, o_ref):
    def branch(y_ref, w_ref, k):
        gate = _sigmoid(gate_ref[:, k * D_MODEL:(k + 1) * D_MODEL])
        return gate * _dot(y_ref[...].astype(BF16), w_ref[...])

    merged = branch(ys5_ref, w0_ref, 0) + branch(ygdn_ref, w1_ref, 1) + branch(yml_ref, w2_ref, 2)
    o_ref[...] = x_ref[...] + _dot(merged.astype(BF16), wo_ref[...])


def _merge(proj, ys5, ygdn, yml, x2d, w0, w1, w2, wo, *, tm):
    M = x2d.shape[0]
    wspec = lambda r: pl.BlockSpec((r, D_MODEL), lambda i: (0, 0))
    return pl.pallas_call(
        _merge_body,
        grid=(M // tm,),
        in_specs=[pl.BlockSpec((tm, 3 * D_MODEL), lambda i: (i, 0)),
                  pl.BlockSpec((tm, 512), lambda i: (i, 0)),
                  pl.BlockSpec((tm, 512), lambda i: (i, 0)),
                  pl.BlockSpec((tm, 512), lambda i: (i, 0)),
                  pl.BlockSpec((tm, D_MODEL), lambda i: (i, 0)),
                  wspec(512), wspec(512), wspec(512), wspec(D_MODEL)],
        out_specs=pl.BlockSpec((tm, D_MODEL), lambda i: (i, 0)),
        out_shape=jax.ShapeDtypeStruct((M, D_MODEL), F32),
        compiler_params=_cparams("parallel"),
        name="merge",
    )(proj, ys5, ygdn, yml, x2d, w0, w1, w2, wo)


def _mlp_body(x_ref, g_ref, wu_ref, wd_ref, p_ref, wp_ref, wpg_ref, gf_ref, o_ref, xn_ref, acc_ref, *, final):
    j = pl.program_id(1)

    @pl.when(j == 0)
    def _():
        xn_ref[...] = _rms(x_ref[...], g_ref[...]).astype(BF16)
        acc_ref[...] = jnp.zeros_like(acc_ref)

    hdn = jnp.maximum(_dot(xn_ref[...], wu_ref[...]), 0.0)
    acc_ref[...] += _dot((hdn * hdn).astype(BF16), wd_ref[...])

    @pl.when(j == pl.num_programs(1) - 1)
    def _():
        x2 = x_ref[...] + acc_ref[...]
        ple = _dot(p_ref[...].astype(BF16), wp_ref[...]) * _sigmoid(_dot(x2.astype(BF16), wpg_ref[...]))
        x3 = x2 + ple
        o_ref[...] = _rms(x3, gf_ref[...]) if final else x3


def _mlp(x2d, g2, wu, wd, p2d, wp, wpg, gf, *, tm, tf, final):
    M = x2d.shape[0]
    return pl.pallas_call(
        functools.partial(_mlp_body, final=final),
        grid=(M // tm, D_FF // tf),
        in_specs=[pl.BlockSpec((tm, D_MODEL), lambda i, j: (i, 0)),
                  pl.BlockSpec((1, D_MODEL), lambda i, j: (0, 0)),
                  pl.BlockSpec((D_MODEL, tf), lambda i, j: (0, j)),
                  pl.BlockSpec((tf, D_MODEL), lambda i, j: (j, 0)),
                  pl.BlockSpec((tm, PLE_DIM), lambda i, j: (i, 0)),
                  pl.BlockSpec((PLE_DIM, D_MODEL), lambda i, j: (0, 0)),
                  pl.BlockSpec((D_MODEL, D_MODEL), lambda i, j: (0, 0)),
                  pl.BlockSpec((1, D_MODEL), lambda i, j: (0, 0))],
        out_specs=pl.BlockSpec((tm, D_MODEL), lambda i, j: (i, 0)),
        out_shape=jax.ShapeDtypeStruct((M, D_MODEL), F32),
        scratch_shapes=[pltpu.VMEM((tm, D_MODEL), BF16), pltpu.VMEM((tm, D_MODEL), F32)],
        compiler_params=_cparams("parallel", "arbitrary"),
        name="mlp",
    )(x2d, g2, wu, wd, p2d, wp, wpg, gf)


def _s5_body(u_ref, h0r_ref, h0i_ref, lr_ref, li_ref, bbr_ref, bbi_ref, cbr_ref, cbi_ref, d_ref, wg_ref, bg_ref,
             y_ref, hfr_ref, hfi_ref, sr_ref, si_ref, bur_ref, bui_ref, *, rows, steps):
    i = pl.program_id(0)
    tb = rows * steps

    @pl.when(i == 0)
    def _():
        sr_ref[...] = h0r_ref[...]
        si_ref[...] = h0i_ref[...]

    u = u_ref[...].reshape(tb, S5_WIDTH)
    ub = u.astype(BF16)
    if steps > 1:
        r_i = lax.broadcasted_iota(jnp.int32, (tb, tb), 0)
        c_i = lax.broadcasted_iota(jnp.int32, (tb, tb), 1)
        to_tb = (c_i == (r_i % rows) * steps + r_i // rows).astype(BF16)
        to_bt = (r_i == (c_i % rows) * steps + c_i // rows).astype(BF16)
        ub = _dot(to_tb, ub).astype(BF16)
    nb = S5_BLOCK_STATES
    for blk in range(S5_BLOCKS):
        ublk = ub[:, blk * 128:(blk + 1) * 128]
        bur_ref[:, blk * nb:(blk + 1) * nb] = _dot(ublk, bbr_ref[blk])
        bui_ref[:, blk * nb:(blk + 1) * nb] = _dot(ublk, bbi_ref[blk])

    for blk in range(S5_BLOCKS):
        cols = slice(blk * nb, (blk + 1) * nb)
        lr = jnp.broadcast_to(lr_ref[:, cols], (rows, nb))
        li = jnp.broadcast_to(li_ref[:, cols], (rows, nb))

        def step(t, carry, cols=cols, lr=lr, li=li):
            hr, hi = carry
            r = pl.ds(pl.multiple_of(t * rows, rows), rows)
            nr = lr * hr - li * hi + bur_ref[r, cols]
            ni = lr * hi + li * hr + bui_ref[r, cols]
            bur_ref[r, cols] = nr
            bui_ref[r, cols] = ni
            return nr, ni

        carry = (sr_ref[:, cols], si_ref[:, cols])
        if steps == 1:
            carry = step(0, carry)
        else:
            carry = lax.fori_loop(0, steps, step, carry, unroll=8)
        sr_ref[:, cols] = carry[0]
        si_ref[:, cols] = carry[1]

    ys = []
    for blk in range(S5_BLOCKS):
        cols = slice(blk * nb, (blk + 1) * nb)
        ys.append(_dot(bur_ref[:, cols].astype(BF16), cbr_ref[blk])
                  - _dot(bui_ref[:, cols].astype(BF16), cbi_ref[blk]))
    y = jnp.concatenate(ys, axis=1)
    if steps > 1:
        y1 = y.astype(BF16)
        r1 = y - y1.astype(F32)
        y2 = r1.astype(BF16)
        y3 = (r1 - y2.astype(F32)).astype(BF16)
        y = _dot(to_bt, y1) + (_dot(to_bt, y2) + _dot(to_bt, y3))
    y = y + d_ref[...] * u
    y = 0.5 * y * (1.0 + jnp.tanh(math.sqrt(2.0 / math.pi) * (y + 0.044715 * (y * y * y))))
    y = y * _sigmoid(_dot(y.astype(BF16), wg_ref[...]) + bg_ref[...])
    y_ref[...] = y.reshape(y_ref.shape)

    @pl.when(i == pl.num_programs(0) - 1)
    def _():
        hfr_ref[...] = sr_ref[...]
        hfi_ref[...] = si_ref[...]


def _s5(proj, h0r, h0i, prm, *, steps):
    rows = proj.shape[0]
    tb = rows * steps
    cu = COL_U // S5_WIDTH
    if proj.ndim == 3:
        nchunks = proj.shape[1] // steps
        u_spec = pl.BlockSpec((rows, steps, S5_WIDTH), lambda i: (0, i, cu))
        y_spec = pl.BlockSpec((rows, steps, S5_WIDTH), lambda i: (0, i, 0))
        y_shape = (rows, proj.shape[1], S5_WIDTH)
    else:
        assert steps == 1
        nchunks = 1
        u_spec = pl.BlockSpec((rows, S5_WIDTH), lambda i: (0, cu))
        y_spec = pl.BlockSpec((rows, S5_WIDTH), lambda i: (0, 0))
        y_shape = (rows, S5_WIDTH)
    full = lambda shape: pl.BlockSpec(shape, lambda i: (0,) * len(shape))
    return pl.pallas_call(
        functools.partial(_s5_body, rows=rows, steps=steps),
        grid=(nchunks,),
        in_specs=[u_spec,
                  full((rows, S5_NSTATE)), full((rows, S5_NSTATE)),
                  full((1, S5_NSTATE)), full((1, S5_NSTATE)),
                  full((S5_BLOCKS, 128, S5_BLOCK_STATES)), full((S5_BLOCKS, 128, S5_BLOCK_STATES)),
                  full((S5_BLOCKS, S5_BLOCK_STATES, 128)), full((S5_BLOCKS, S5_BLOCK_STATES, 128)),
                  full((1, S5_WIDTH)), full((S5_WIDTH, S5_WIDTH)), full((1, S5_WIDTH))],
        out_specs=[y_spec, full((rows, S5_NSTATE)), full((rows, S5_NSTATE))],
        out_shape=[jax.ShapeDtypeStruct(y_shape, F32),
                   jax.ShapeDtypeStruct((rows, S5_NSTATE), F32),
                   jax.ShapeDtypeStruct((rows, S5_NSTATE), F32)],
        scratch_shapes=[pltpu.VMEM((rows, S5_NSTATE), F32), pltpu.VMEM((rows, S5_NSTATE), F32),
                        pltpu.VMEM((tb, S5_NSTATE), F32), pltpu.VMEM((tb, S5_NSTATE), F32)],
        compiler_params=_cparams("arbitrary"),
        name="s5",
    )(proj, h0r, h0i, prm["lam_re"], prm["lam_im"], prm["bbd_re"], prm["bbd_im"],
      prm["cbd_re"], prm["cbd_im"], prm["d"], prm["w_glu"], prm["b_glu"])


def _s5_params(A_re, A_im, log_dt, B_re, B_im, C_re, C_im, d_skip, w_glu, b_glu):
    dt = jnp.exp(log_dt)[:, None]
    mag = jnp.exp(A_re * dt)
    lam_re, lam_im = mag * jnp.cos(A_im * dt), mag * jnp.sin(A_im * dt)
    inv = 1.0 / (A_re * A_re + A_im * A_im)
    zr, zi = lam_re - 1.0, lam_im
    fac_re = (zr * A_re + zi * A_im) * inv
    fac_im = (zi * A_re - zr * A_im) * inv
    bb_re = fac_re[..., None] * B_re - fac_im[..., None] * B_im
    bb_im = fac_re[..., None] * B_im + fac_im[..., None] * B_re
    gpb = S5_GROUPS // S5_BLOCKS
    eye = jnp.eye(gpb, dtype=F32)

    def in_map(bb):
        t = bb.reshape(S5_BLOCKS, gpb, S5_STATE, S5_GROUP).transpose(0, 1, 3, 2)
        return jnp.einsum('bgcp,gh->bgchp', t, eye).reshape(S5_BLOCKS, 128, S5_BLOCK_STATES).astype(BF16)

    def out_map(cc):
        t = cc.reshape(S5_BLOCKS, gpb, S5_GROUP, S5_STATE).transpose(0, 1, 3, 2)
        return jnp.einsum('bgpc,gh->bgphc', t, eye).reshape(S5_BLOCKS, S5_BLOCK_STATES, 128).astype(BF16)

    return dict(lam_re=lam_re.reshape(1, S5_NSTATE), lam_im=lam_im.reshape(1, S5_NSTATE),
                bbd_re=in_map(bb_re), bbd_im=in_map(bb_im), cbd_re=out_map(C_re), cbd_im=out_map(C_im),
                d=d_skip.reshape(1, S5_WIDTH), w_glu=w_glu.astype(BF16), b_glu=b_glu.reshape(1, S5_WIDTH))


def _tri_masks(c):
    row = lax.broadcasted_iota(jnp.int32, (c, c), 0)
    col = lax.broadcasted_iota(jnp.int32, (c, c), 1)
    return row >= col, row > col


def _head_block_masks(c, heads):
    n = c * heads
    row = lax.broadcasted_iota(jnp.int32, (n, n), 0)
    col = lax.broadcasted_iota(jnp.int32, (n, n), 1)
    same = (row // c) == (col // c)
    return same & (row >= col), same & (row > col)


def _split(a):
    hi = a.astype(BF16)
    return hi, (a - hi.astype(F32)).astype(BF16)


def _gdn_body(qkv_ref, z_ref, sm_ref, cb0_ref, s0_ref, cw_ref, sb_ref, nega_ref, ng_ref,
              o_ref, cbo_ref, so_ref, xs_ref, s_ref, *, c, nc):
    j = pl.program_id(1)
    tb = c * nc

    @pl.when(j == 0)
    def _():
        xs_ref[5:8, :] = cb0_ref[...]
        s_ref[...] = s0_ref[...]

    xs_ref[8:8 + tb, :] = qkv_ref[...]

    incl, _ = _tri_masks(c)
    tri = incl.astype(F32)
    incl_bd, strict_bd = _head_block_masks(c, GDN_HEADS)
    cw = cw_ref[...]
    nsteps = max(1, int(math.ceil(math.log2(c))))

    for ci in range(nc):
        r0 = 8 + ci * c
        conv = xs_ref[r0 - 3:r0 - 3 + c, :] * cw[0:1]
        conv = conv + xs_ref[r0 - 2:r0 - 2 + c, :] * cw[1:2]
        conv = conv + xs_ref[r0 - 1:r0 - 1 + c, :] * cw[2:3]
        conv = conv + xs_ref[r0:r0 + c, :] * cw[3:4]
        conv = conv * _sigmoid(conv)
        rows = slice(ci * c, (ci + 1) * c)
        zz = z_ref[rows, :]
        pre = sm_ref[rows, :] + sb_ref[...]
        beta_all = _sigmoid(pre)
        g_all = nega_ref[...] * _softplus(pre)
        gcum = _dot_hi(tri, g_all)
        gcum_t = gcum.T
        H = GDN_HEADS
        stack = lambda f: jnp.concatenate([f(h) for h in range(H)], axis=0)
        q = stack(lambda h: conv[:, 128 * h:128 * (h + 1)])
        k = stack(lambda h: conv[:, 512 + 128 * h:512 + 128 * (h + 1)])
        v = stack(lambda h: conv[:, 1024 + 128 * h:1024 + 128 * (h + 1)])
        q = q * lax.rsqrt(jnp.sum(q * q, axis=-1, keepdims=True) + EPS) * (GDN_DK ** -0.5)
        k = k * lax.rsqrt(jnp.sum(k * k, axis=-1, keepdims=True) + EPS)
        beta = stack(lambda h: beta_all[:, LANE_BETA + h:LANE_BETA + h + 1])
        gc = stack(lambda h: gcum[:, LANE_G + h:LANE_G + h + 1])
        g_end = stack(lambda h: jnp.broadcast_to(gcum[c - 1:c, LANE_G + h:LANE_G + h + 1], (c, 1)))
        gr = jnp.concatenate([gcum_t[LANE_G + h:LANE_G + h + 1, :] for h in range(H)], axis=1)
        decay = jnp.exp(jnp.where(incl_bd, gc - gr, -jnp.inf))
        kb = k * beta
        a_low = jnp.where(strict_bd, _dot_nt(kb, k) * decay, 0.0)
        eg = jnp.exp(gc)
        x = jnp.concatenate([v * beta, kb * eg], axis=1)
        m = -a_low
        for it in range(nsteps):
            mh, ml = _split(m)
            xh, xl = _split(x)
            x = x + (_dot(mh, xh) + (_dot(mh, xl) + _dot(ml, xh)))
            if it < nsteps - 1:
                m = _dot(mh, mh) + (_dot(mh, ml) + _dot(ml, mh))
        uu, ww = x[:, :GDN_DV], x[:, GDN_DV:]
        attn = jnp.where(incl_bd, _dot_nt(q, k) * decay, 0.0)
        q_dec = q * eg
        k_dec = k * jnp.exp(g_end - gc)
        hrows = lambda a, h: a[c * h:c * (h + 1)]
        s_old = [s_ref[h] for h in range(H)]
        v_new = uu - stack(lambda h: _dot(hrows(ww, h), s_old[h]))
        o = stack(lambda h: _dot(hrows(q_dec, h), s_old[h])) + _dot(attn, v_new)
        s_scale = jnp.exp(g_end)
        for h in range(H):
            s_ref[h] = s_old[h] * s_scale[c * h:c * h + 1] + _dot_tn(hrows(k_dec, h), hrows(v_new, h))
        o = o * lax.rsqrt(jnp.mean(o * o, axis=-1, keepdims=True) + EPS) * ng_ref[...]
        zs = stack(lambda h: zz[:, 128 * h:128 * (h + 1)])
        o = o * (zs * _sigmoid(zs))
        for h in range(H):
            o_ref[rows, 128 * h:128 * (h + 1)] = o[c * h:c * (h + 1)]

    tail = xs_ref[8 + tb - 3:8 + tb, :]
    xs_ref[5:8, :] = tail
    cbo_ref[...] = tail

    @pl.when(j == pl.num_programs(1) - 1)
    def _():
        so_ref[...] = s_ref[...]


def _gdn(proj3, cb0, s0, prm, *, c, nc):
    B, Lr, _ = proj3.shape
    valid = c * nc
    nblk = Lr // valid
    cq, cz, cs = COL_QKV // 1536, COL_Z // 512, COL_SMALL // 128
    return pl.pallas_call(
        functools.partial(_gdn_body, c=c, nc=nc),
        grid=(B, nblk),
        in_specs=[pl.BlockSpec((None, valid, 1536), lambda b, j: (b, j, cq)),
                  pl.BlockSpec((None, valid, 512), lambda b, j: (b, j, cz)),
                  pl.BlockSpec((None, valid, 128), lambda b, j: (b, j, cs)),
                  pl.BlockSpec((None, 3, GDN_CONV_CH), lambda b, j: (b, 0, 0)),
                  pl.BlockSpec((None, GDN_HEADS, GDN_DK, GDN_DV), lambda b, j: (b, 0, 0, 0)),
                  pl.BlockSpec((GDN_CONV_W, GDN_CONV_CH), lambda b, j: (0, 0)),
                  pl.BlockSpec((1, 128), lambda b, j: (0, 0)),
                  pl.BlockSpec((1, 128), lambda b, j: (0, 0)),
                  pl.BlockSpec((1, 128), lambda b, j: (0, 0))],
        out_specs=[pl.BlockSpec((None, valid, 512), lambda b, j: (b, j, 0)),
                   pl.BlockSpec((None, 3, GDN_CONV_CH), lambda b, j: (b, 0, 0)),
                   pl.BlockSpec((None, GDN_HEADS, GDN_DK, GDN_DV), lambda b, j: (b, 0, 0, 0))],
        out_shape=[jax.ShapeDtypeStruct((B, Lr, 512), F32),
                   jax.ShapeDtypeStruct((B, 3, GDN_CONV_CH), F32),
                   jax.ShapeDtypeStruct((B, GDN_HEADS, GDN_DK, GDN_DV), F32)],
        scratch_shapes=[pltpu.VMEM((8 + c * nc, GDN_CONV_CH), F32),
                        pltpu.VMEM((GDN_HEADS, GDN_DK, GDN_DV), F32)],
        compiler_params=_cparams("parallel", "arbitrary"),
        name="gdn",
    )(proj3, proj3, proj3, cb0, s0, prm["conv_w"], prm["bias"], prm["nega"], prm["norm_g"])


def _gdn_step_body(qkv_ref, z_ref, sm_ref, cb_ref, s_ref, cw_ref, sb_ref, nega_ref, ng_ref,
                   o_ref, cbo_ref, so_ref, *, nb):
    x = qkv_ref[...]
    cw = cw_ref[...]
    b0, b1, b2 = cb_ref[0], cb_ref[1], cb_ref[2]
    conv = ((b0 * cw[0:1] + b1 * cw[1:2]) + b2 * cw[2:3]) + x * cw[3:4]
    cbo_ref[0] = b1
    cbo_ref[1] = b2
    cbo_ref[2] = x
    conv = conv * _sigmoid(conv)
    pre = sm_ref[...] + sb_ref[...]
    beta_all = _sigmoid(pre)
    eg_all = jnp.exp(nega_ref[...] * _softplus(pre))
    zz = z_ref[...]
    for h in range(GDN_HEADS):
        q = conv[:, 128 * h:128 * (h + 1)]
        k = conv[:, 512 + 128 * h:512 + 128 * (h + 1)]
        v = conv[:, 1024 + 128 * h:1024 + 128 * (h + 1)]
        q = q * lax.rsqrt(jnp.sum(q * q, axis=-1, keepdims=True) + EPS) * (GDN_DK ** -0.5)
        k = k * lax.rsqrt(jnp.sum(k * k, axis=-1, keepdims=True) + EPS)
        beta = beta_all[:, LANE_BETA + h:LANE_BETA + h + 1]
        eg = eg_all[:, LANE_G + h:LANE_G + h + 1]
        kb = k * beta
        uu = v * beta
        attn = jnp.sum(q * k, axis=-1, keepdims=True)
        w_t = (kb * eg).T
        q_t = (q * eg).T
        k_t = k.T
        rows = []
        for b in range(nb):
            s_old = s_ref[b, h]
            ws = jnp.sum(s_old * w_t[:, b:b + 1], axis=0, keepdims=True)
            qs = jnp.sum(s_old * q_t[:, b:b + 1], axis=0, keepdims=True)
            v_new = uu[b:b + 1] - ws
            rows.append(qs + attn[b:b + 1] * v_new)
            so_ref[b, h] = s_old * eg[b:b + 1] + k_t[:, b:b + 1] * v_new
        o = jnp.concatenate(rows, axis=0)
        o = o * lax.rsqrt(jnp.mean(o * o, axis=-1, keepdims=True) + EPS) * ng_ref[...]
        zh = zz[:, 128 * h:128 * (h + 1)]
        o_ref[:, 128 * h:128 * (h + 1)] = o * (zh * _sigmoid(zh))


def _gdn_step(proj, cb0, s0, prm, *, nb):
    B = proj.shape[0]
    cq, cz, cs = COL_QKV // 1536, COL_Z // 512, COL_SMALL // 128
    vec = pl.BlockSpec((1, 128), lambda i: (0, 0))
    o, cbn, sn = pl.pallas_call(
        functools.partial(_gdn_step_body, nb=nb),
        grid=(B // nb,),
        in_specs=[pl.BlockSpec((nb, 1536), lambda i: (i, cq)),
                  pl.BlockSpec((nb, 512), lambda i: (i, cz)),
                  pl.BlockSpec((nb, 128), lambda i: (i, cs)),
                  pl.BlockSpec((3, nb, GDN_CONV_CH), lambda i: (0, i, 0)),
                  pl.BlockSpec((nb, GDN_HEADS, GDN_DK, GDN_DV), lambda i: (i, 0, 0, 0)),
                  pl.BlockSpec((GDN_CONV_W, GDN_CONV_CH), lambda i: (0, 0)), vec, vec, vec],
        out_specs=[pl.BlockSpec((nb, 512), lambda i: (i, 0)),
                   pl.BlockSpec((3, nb, GDN_CONV_CH), lambda i: (0, i, 0)),
                   pl.BlockSpec((nb, GDN_HEADS, GDN_DK, GDN_DV), lambda i: (i, 0, 0, 0))],
        out_shape=[jax.ShapeDtypeStruct((B, 512), F32),
                   jax.ShapeDtypeStruct((3, B, GDN_CONV_CH), F32),
                   jax.ShapeDtypeStruct((B, GDN_HEADS, GDN_DK, GDN_DV), F32)],
        compiler_params=_cparams("parallel"),
        name="gdn_step",
    )(proj, proj, proj, jnp.swapaxes(cb0, 0, 1), s0, prm["conv_w"], prm["bias"], prm["nega"], prm["norm_g"])
    return o, jnp.swapaxes(cbn, 0, 1), sn


def _mlstm_body(q_ref, k_ref, v_ref, og_ref, sm_ref, c0_ref, n0_ref, m0_ref, sb_ref, ng_ref,
                o_ref, co_ref, no_ref, mo_ref, c_ref, n_ref, m_ref, *, c, nc):
    j = pl.program_id(1)
    H = ML_HEADS

    @pl.when(j == 0)
    def _():
        c_ref[...] = c0_ref[...]
        n_ref[...] = n0_ref[...]
        m_ref[...] = m0_ref[...]

    incl, _ = _tri_masks(c)
    tri = incl.astype(F32)
    srow_t = lax.broadcasted_iota(jnp.int32, (H * c, c), 0) % c
    scol = lax.broadcasted_iota(jnp.int32, (H * c, c), 1)
    incl_st = srow_t >= scol
    lane = lax.broadcasted_iota(jnp.int32, (1, ML_QK_W), 1)
    srow = lax.broadcasted_iota(jnp.int32, (ML_QK_W, 1), 0)
    hmask = [((lane >= ML_DK * h) & (lane < ML_DK * (h + 1))).astype(F32) for h in range(H)]
    rmask = [((srow >= ML_DK * h) & (srow < ML_DK * (h + 1))).astype(F32) for h in range(H)]
    stack = lambda f: jnp.concatenate([f(h) for h in range(H)], axis=0)
    col = lambda a: jnp.broadcast_to(a, (c, 1))

    for ci in range(nc):
        rows = slice(ci * c, (ci + 1) * c)
        q, v, og = q_ref[rows, :], v_ref[rows, :], og_ref[rows, :]
        k = k_ref[rows, :] * (ML_DK ** -0.5)
        pre = sm_ref[rows, :] + sb_ref[...]
        li_all = GATE_CAP * jnp.tanh(pre / GATE_CAP)
        lf_all = -_softplus(-li_all)
        bcum = _dot_hi(tri, lf_all)
        bcum_t = bcum.T
        li_t = li_all.T
        c_old = c_ref[...]
        n_old = n_ref[...]
        bc = stack(lambda h: bcum[:, LANE_F + h:LANE_F + h + 1])
        lic = stack(lambda h: li_all[:, LANE_I + h:LANE_I + h + 1])
        bc_end = stack(lambda h: col(bcum[c - 1:c, LANE_F + h:LANE_F + h + 1]))
        m_prev = stack(lambda h: col(m_ref[h:h + 1, 0:1]))
        rowterm = stack(lambda h: jnp.broadcast_to(
            li_t[LANE_I + h:LANE_I + h + 1, :] - bcum_t[LANE_F + h:LANE_F + h + 1, :], (c, c)))
        logw = jnp.where(incl_st, bc + rowterm, -jnp.inf)
        inter = bc + m_prev
        m_t = jnp.maximum(inter, jnp.max(logw, axis=1, keepdims=True))
        wts = jnp.exp(logw - m_t)
        sc = jnp.exp(inter - m_t)
        qs = stack(lambda h: q * hmask[h])
        s_qk = _dot_nt(qs, k) * wts
        sv = _dot(s_qk, v)
        num = stack(lambda h: sv[c * h:c * (h + 1), ML_DV * h:ML_DV * (h + 1)]) + sc * _dot(qs, c_old)
        den = jnp.sum(s_qk, axis=1, keepdims=True) + sc * jnp.sum(qs * n_old, axis=1, keepdims=True)
        hh = num / jnp.maximum(jnp.abs(den), jnp.exp(-m_t))
        m_end = [m_t[c * h + c - 1:c * h + c, :] for h in range(H)]
        sc_end = [sc[c * h + c - 1:c * h + c, :] for h in range(H)]
        w_last = jnp.exp(bc_end - bc + lic - stack(lambda h: col(m_end[h])))
        kw = stack(lambda h: k * hmask[h]) * w_last
        vs = stack(lambda h: v[:, ML_DV * h:ML_DV * (h + 1)])
        sc_rows = sum(sc_end[h] * rmask[h] for h in range(H))
        sc_lanes = sum(sc_end[h] * hmask[h] for h in range(H))
        c_ref[...] = sc_rows * c_old + _dot_tn(kw, vs)
        n_ref[...] = sc_lanes * n_old + jnp.sum(kw, axis=0, keepdims=True)
        hh = hh * lax.rsqrt(jnp.mean(hh * hh, axis=-1, keepdims=True) + EPS)
        for h in range(H):
            m_ref[h:h + 1, :] = jnp.broadcast_to(m_end[h], (1, 128))
            hs = slice(ML_DV * h, ML_DV * (h + 1))
            o_ref[rows, hs] = hh[c * h:c * (h + 1)] * ng_ref[:, hs] * _sigmoid(og[:, hs])

    @pl.when(j == pl.num_programs(1) - 1)
    def _():
        co_ref[...] = c_ref[...]
        no_ref[...] = n_ref[...]
        mo_ref[...] = m_ref[...]


def _mlstm(proj3, c0, n0, m0, prm, *, c, nc):
    B, Lr, _ = proj3.shape
    valid = c * nc
    nblk = Lr // valid
    cq, ck, cv, co, cs = COL_MQ // 256, COL_MK // 256, COL_MV // 512, COL_MO // 512, COL_SMALL // 128
    st = lambda shape: pl.BlockSpec((None,) + shape, lambda b, j: (b,) + (0,) * len(shape))
    return pl.pallas_call(
        functools.partial(_mlstm_body, c=c, nc=nc),
        grid=(B, nblk),
        in_specs=[pl.BlockSpec((None, valid, 256), lambda b, j: (b, j, cq)),
                  pl.BlockSpec((None, valid, 256), lambda b, j: (b, j, ck)),
                  pl.BlockSpec((None, valid, 512), lambda b, j: (b, j, cv)),
                  pl.BlockSpec((None, valid, 512), lambda b, j: (b, j, co)),
                  pl.BlockSpec((None, valid, 128), lambda b, j: (b, j, cs)),
                  st((ML_QK_W, ML_DV)), st((1, ML_QK_W)), st((8, 128)),
                  pl.BlockSpec((1, 128), lambda b, j: (0, 0)),
                  pl.BlockSpec((1, 512), lambda b, j: (0, 0))],
        out_specs=[pl.BlockSpec((None, valid, 512), lambda b, j: (b, j, 0)),
                   st((ML_QK_W, ML_DV)), st((1, ML_QK_W)), st((8, 128))],
        out_shape=[jax.ShapeDtypeStruct((B, Lr, 512), F32),
                   jax.ShapeDtypeStruct((B, ML_QK_W, ML_DV), F32),
                   jax.ShapeDtypeStruct((B, 1, ML_QK_W), F32),
                   jax.ShapeDtypeStruct((B, 8, 128), F32)],
        scratch_shapes=[pltpu.VMEM((ML_QK_W, ML_DV), F32), pltpu.VMEM((1, ML_QK_W), F32),
                        pltpu.VMEM((8, 128), F32)],
        compiler_params=_cparams("parallel", "arbitrary"),
        name="mlstm",
    )(proj3, proj3, proj3, proj3, proj3, c0, n0, m0, prm["bias"], prm["norm_g"])


def _mlstm_step_body(q_ref, k_ref, v_ref, og_ref, sm_ref, c_ref, n_ref, m_ref, sb_ref, ng_ref,
                     o_ref, co_ref, no_ref, mo_ref, *, nb):
    q = q_ref[...]
    k = k_ref[...] * (ML_DK ** -0.5)
    v = v_ref[...]
    og = og_ref[...]
    n_old = n_ref[...]
    m_old = m_ref[...]
    pre = sm_ref[...] + sb_ref[...]
    capped = GATE_CAP * jnp.tanh(pre / GATE_CAP)
    lf_all = -_softplus(-capped)
    lane = lax.broadcasted_iota(jnp.int32, (1, ML_QK_W), 1)
    lane128 = lax.broadcasted_iota(jnp.int32, (1, 128), 1)
    q_t = q.T
    k_t = k.T
    n_new = jnp.zeros((nb, ML_QK_W), F32)
    m_new = jnp.zeros((nb, 128), F32)
    for h in range(ML_HEADS):
        hmask = ((lane >= ML_DK * h) & (lane < ML_DK * (h + 1))).astype(F32)
        li = capped[:, LANE_I + h:LANE_I + h + 1]
        inter = lf_all[:, LANE_F + h:LANE_F + h + 1] + m_old[:, h:h + 1]
        m_t = jnp.maximum(inter, li)
        wts = jnp.exp(li - m_t)
        sc = jnp.exp(inter - m_t)
        qh = q * hmask
        s_qk = jnp.sum(qh * k, axis=-1, keepdims=True) * wts
        den = s_qk + sc * jnp.sum(qh * n_old, axis=-1, keepdims=True)
        vh = v[:, ML_DV * h:ML_DV * (h + 1)]
        hs = slice(ML_DK * h, ML_DK * (h + 1))
        rows = []
        for b in range(nb):
            c_old = c_ref[b, hs, :]
            rows.append(jnp.sum(c_old * q_t[hs, b:b + 1], axis=0, keepdims=True))
            co_ref[b, hs, :] = sc[b:b + 1] * c_old + (wts[b:b + 1] * k_t[hs, b:b + 1]) * vh[b:b + 1]
        num = s_qk * vh + sc * jnp.concatenate(rows, axis=0)
        hh = num / jnp.maximum(jnp.abs(den), jnp.exp(-m_t))
        hh = hh * lax.rsqrt(jnp.mean(hh * hh, axis=-1, keepdims=True) + EPS) * ng_ref[:, ML_DV * h:ML_DV * (h + 1)]
        o_ref[:, ML_DV * h:ML_DV * (h + 1)] = hh * _sigmoid(og[:, ML_DV * h:ML_DV * (h + 1)])
        n_new = n_new + hmask * (sc * n_old + wts * k)
        m_new = m_new + m_t * (lane128 == h).astype(F32)
    no_ref[...] = n_new
    mo_ref[...] = m_new


def _mlstm_step(proj, c0, n0, m0, prm, *, nb):
    B = proj.shape[0]
    cq, ck, cv, co, cs = COL_MQ // 256, COL_MK // 256, COL_MV // 512, COL_MO // 512, COL_SMALL // 128
    return pl.pallas_call(
        functools.partial(_mlstm_step_body, nb=nb),
        grid=(B // nb,),
        in_specs=[pl.BlockSpec((nb, 256), lambda i: (i, cq)),
                  pl.BlockSpec((nb, 256), lambda i: (i, ck)),
                  pl.BlockSpec((nb, 512), lambda i: (i, cv)),
                  pl.BlockSpec((nb, 512), lambda i: (i, co)),
                  pl.BlockSpec((nb, 128), lambda i: (i, cs)),
                  pl.BlockSpec((nb, ML_QK_W, ML_DV), lambda i: (i, 0, 0)),
                  pl.BlockSpec((nb, ML_QK_W), lambda i: (i, 0)),
                  pl.BlockSpec((nb, 128), lambda i: (i, 0)),
                  pl.BlockSpec((1, 128), lambda i: (0, 0)),
                  pl.BlockSpec((1, 512), lambda i: (0, 0))],
        out_specs=[pl.BlockSpec((nb, 512), lambda i: (i, 0)),
                   pl.BlockSpec((nb, ML_QK_W, ML_DV), lambda i: (i, 0, 0)),
                   pl.BlockSpec((nb, ML_QK_W), lambda i: (i, 0)),
                   pl.BlockSpec((nb, 128), lambda i: (i, 0))],
        out_shape=[jax.ShapeDtypeStruct((B, 512), F32),
                   jax.ShapeDtypeStruct((B, ML_QK_W, ML_DV), F32),
                   jax.ShapeDtypeStruct((B, ML_QK_W), F32),
                   jax.ShapeDtypeStruct((B, 128), F32)],
        compiler_params=_cparams("parallel"),
        name="mlstm_step",
    )(proj, proj, proj, proj, proj, c0, n0, m0, prm["bias"], prm["norm_g"])


def _lane_vec(pieces):
    v = jnp.zeros((128,), F32)
    for off, val in pieces:
        v = v.at[off:off + val.shape[0]].set(val.astype(F32))
    return v.reshape(1, 128)


def _layer_params(i, w):
    win = w["w_in"][i]
    o = [0, 512, 2048, 2560, 2564, 2568, 2824, 3080, 3592, 4104, 4108, 4112, 7184]
    seg = lambda a: win[:, o[a]:o[a + 1]]
    w_u, w_qkv, w_z, w_b, w_a, w_mq, w_mk, w_mv, w_mo, w_mi, w_mf, w_gate = (seg(a) for a in range(12))
    small = jnp.concatenate([w_b, w_a, w_mi, w_mf, jnp.zeros((D_MODEL, 128 - 16), F32)], axis=1)
    w_main = jnp.concatenate([w_gate, w_qkv, w_z, w_mv, w_mo, w_u, w_mq, w_mk, small], axis=1).astype(BF16)
    bias = _lane_vec([(LANE_G, w["gdn_dt_bias"][i]), (LANE_I, w["ml_b_i"][i]), (LANE_F, w["ml_b_f"][i])])
    return dict(
        norm1_g=w["norm1_g"][i].reshape(1, D_MODEL), w_main=w_main,
        s5=_s5_params(w["s5_A_re"][i], w["s5_A_im"][i], w["s5_log_dt"][i], w["s5_B_re"][i], w["s5_B_im"][i],
                      w["s5_C_re"][i], w["s5_C_im"][i], w["s5_D"][i], w["s5_w_glu"][i], w["s5_b_glu"][i]),
        gdn=dict(conv_w=w["gdn_conv_w"][i], bias=bias, nega=_lane_vec([(LANE_G, -jnp.exp(w["gdn_A_log"][i]))]),
                 norm_g=w["gdn_norm_g"][i].reshape(1, GDN_DV)),
        ml=dict(bias=bias, norm_g=w["ml_norm_g"][i].reshape(1, 512)),
        w_br_s5=w["w_br_s5"][i].astype(BF16), w_br_gdn=w["w_br_gdn"][i].astype(BF16),
        w_br_ml=w["w_br_ml"][i].astype(BF16), w_out=w["w_out"][i].astype(BF16),
        norm2_g=w["norm2_g"][i].reshape(1, D_MODEL), w_up=w["w_up"][i].astype(BF16),
        w_down=w["w_down"][i].astype(BF16), w_ple=w["w_ple"][i].astype(BF16),
        w_ple_gate=w["w_ple_gate"][i].astype(BF16))


def _block(x2d, p2d, B, L, state, lp, gf, final):
    s5r, s5i, cbuf, gS, mC, mn, mm = state
    prompt = L > 1
    tm, tmm, tml, tf = (1024, 256, 1024, 1024) if prompt else (B, B, B, 1024)
    proj = _inproj(x2d, lp["norm1_g"], lp["w_main"], tm=tm, tn=PROJ_TN)
    h0 = (s5r.reshape(B, S5_NSTATE), s5i.reshape(B, S5_NSTATE))
    if prompt:
        proj3 = proj.reshape(B, L, PROJ_W)
        y_s5, h_re, h_im = _s5(proj3, *h0, lp["s5"], steps=CHUNK)
        y_gdn, cbuf_n, gS_n = _gdn(proj3, cbuf, gS, lp["gdn"], c=CHUNK, nc=1)
        m_in = jnp.broadcast_to(jnp.pad(mm, ((0, 0), (0, 8 - ML_HEADS)))[:, :, None], (B, 8, 128))
        y_ml, mC_n, mn_n, mm_n = _mlstm(proj3, mC.reshape(B, ML_QK_W, ML_DV), mn.reshape(B, 1, ML_QK_W), m_in,
                                        lp["ml"], c=CHUNK, nc=1)
        mm_n = mm_n[:, :ML_HEADS, 0]
    else:
        y_s5, h_re, h_im = _s5(proj, *h0, lp["s5"], steps=1)
        y_gdn, cbuf_n, gS_n = _gdn_step(proj, cbuf, gS, lp["gdn"], nb=STEP_ROWS)
        y_ml, mC_n, mn_n, mm_n = _mlstm_step(proj, mC.reshape(B, ML_QK_W, ML_DV), mn.reshape(B, ML_QK_W),
                                             jnp.pad(mm, ((0, 0), (0, 128 - ML_HEADS))), lp["ml"], nb=STEP_ROWS)
        mm_n = mm_n[:, :ML_HEADS]
    x2d = _merge(proj, y_s5.reshape(B * L, 512), y_gdn.reshape(B * L, 512), y_ml.reshape(B * L, 512), x2d,
                 lp["w_br_s5"], lp["w_br_gdn"], lp["w_br_ml"], lp["w_out"], tm=tmm)
    x2d = _mlp(x2d, lp["norm2_g"], lp["w_up"], lp["w_down"], p2d, lp["w_ple"], lp["w_ple_gate"], gf,
               tm=tml, tf=tf, final=final)
    new_state = (h_re.reshape(B, S5_GROUPS, S5_STATE), h_im.reshape(B, S5_GROUPS, S5_STATE), cbuf_n, gS_n,
                 mC_n.reshape(B, ML_HEADS, ML_DK, ML_DV), mn_n.reshape(B, ML_HEADS, ML_DK), mm_n)
    return x2d, new_state


def kernel(x_prompt, x_sample, state_s5_re, state_s5_im, state_gdn_conv, state_gdn, state_mlstm_C, state_mlstm_n, state_mlstm_m, p_prompt, p_sample, norm1_g, w_in, s5_A_re, s5_A_im, s5_log_dt, s5_B_re, s5_B_im, s5_C_re, s5_C_im, s5_D, s5_w_glu, s5_b_glu, gdn_conv_w, gdn_A_log, gdn_dt_bias, gdn_norm_g, ml_b_i, ml_b_f, ml_norm_g, w_br_s5, w_br_gdn, w_br_ml, w_out, norm2_g, w_up, w_down, w_ple, w_ple_gate, final_norm_g):
    w = dict(norm1_g=norm1_g, w_in=w_in, s5_A_re=s5_A_re, s5_A_im=s5_A_im, s5_log_dt=s5_log_dt, s5_B_re=s5_B_re,
             s5_B_im=s5_B_im, s5_C_re=s5_C_re, s5_C_im=s5_C_im, s5_D=s5_D, s5_w_glu=s5_w_glu, s5_b_glu=s5_b_glu,
             gdn_conv_w=gdn_conv_w, gdn_A_log=gdn_A_log, gdn_dt_bias=gdn_dt_bias, gdn_norm_g=gdn_norm_g,
             ml_b_i=ml_b_i, ml_b_f=ml_b_f, ml_norm_g=ml_norm_g, w_br_s5=w_br_s5, w_br_gdn=w_br_gdn,
             w_br_ml=w_br_ml, w_out=w_out, norm2_g=norm2_g, w_up=w_up, w_down=w_down, w_ple=w_ple,
             w_ple_gate=w_ple_gate)
    bp, lp_len, _ = x_prompt.shape
    bs, ls_len, _ = x_sample.shape
    gf = final_norm_g.reshape(1, D_MODEL)
    xp = x_prompt.reshape(bp * lp_len, D_MODEL)
    xs = x_sample.reshape(bs * ls_len, D_MODEL)
    new_p, new_s = [], []
    for i in range(DEPTH):
        lp = _layer_params(i, w)
        zero_state = (jnp.zeros((bp, S5_GROUPS, S5_STATE), F32), jnp.zeros((bp, S5_GROUPS, S5_STATE), F32),
                      jnp.zeros((bp, GDN_CONV_W - 1, GDN_CONV_CH), F32),
                      jnp.zeros((bp, GDN_HEADS, GDN_DK, GDN_DV), F32),
                      jnp.zeros((bp, ML_HEADS, ML_DK, ML_DV), F32), jnp.zeros((bp, ML_HEADS, ML_DK), F32),
                      jnp.zeros((bp, ML_HEADS), F32))
        final = i == DEPTH - 1
        xp, sp_i = _block(xp, p_prompt[i].reshape(bp * lp_len, PLE_DIM), bp, lp_len, zero_state, lp, gf, final)
        sample_state = (state_s5_re[i], state_s5_im[i], state_gdn_conv[i], state_gdn[i], state_mlstm_C[i],
                        state_mlstm_n[i], state_mlstm_m[i])
        xs, ss_i = _block(xs, p_sample[i].reshape(bs * ls_len, PLE_DIM), bs, ls_len, sample_state, lp, gf, final)
        new_p.append(sp_i)
        new_s.append(ss_i)
    sp = [jnp.stack(t) for t in zip(*new_p)]
    ss = [jnp.stack(t) for t in zip(*new_s)]
    return (xp.reshape(bp, lp_len, D_MODEL), xs.reshape(bs, ls_len, D_MODEL),
            sp[0], ss[0], sp[1], ss[1], sp[2], ss[2], sp[3], ss[3], sp[4], ss[4], sp[5], ss[5], sp[6], ss[6])
```

```python
import functools
import math

import jax
import jax.numpy as jnp
from jax import lax
from jax.experimental import pallas as pl
from jax.experimental.pallas import tpu as pltpu

F32 = jnp.float32
BF16 = jnp.bfloat16
HI = lax.Precision.HIGHEST

D_MODEL = 1024
DEPTH = 2
PLE_DIM = 256
D_FF = 4 * D_MODEL
EPS = 1e-6
CHUNK = 64

S5_WIDTH = 512
S5_GROUP = 16
S5_GROUPS = 32
S5_STATE = 64
S5_NSTATE = S5_GROUPS * S5_STATE
S5_BLOCKS = 4
S5_BLOCK_STATES = S5_NSTATE // S5_BLOCKS

GDN_HEADS = 4
GDN_DK = 128
GDN_DV = 128
GDN_CONV_W = 4
GDN_CONV_CH = 1536

ML_HEADS = 4
ML_DK = 64
ML_DV = 128
ML_QK_W = ML_HEADS * ML_DK
GATE_CAP = 15.0

PROJ_W = 4224
COL_QKV = 0
COL_Z = 1536
COL_MV = 2048
COL_MO = 2560
COL_U = 3072
COL_MQ = 3584
COL_MK = 3840
COL_SMALL = 4096
LANE_BETA, LANE_G, LANE_I, LANE_F = 0, 4, 8, 12

STEP_ROWS = 8
MIXER_CHUNKS_PER_STEP = 4

VMEM_LIMIT = 52 * 1024 * 1024


def _cparams(*sem):
    return pltpu.CompilerParams(dimension_semantics=sem, vmem_limit_bytes=VMEM_LIMIT)


def _dot(a, b):
    return jnp.dot(a, b, preferred_element_type=F32)


def _dot_hi(a, b):
    return jnp.dot(a, b, preferred_element_type=F32, precision=HI)


def _dot_nt(a, b):
    return lax.dot_general(a, b, (((1,), (1,)), ((), ())), preferred_element_type=F32)


def _dot_tn(a, b):
    return lax.dot_general(a, b, (((0,), (0,)), ((), ())), preferred_element_type=F32)


def _sigmoid(x):
    return 1.0 / (1.0 + jnp.exp(-x))


def _softplus(x):
    return jnp.maximum(x, 0.0) + jnp.log(1.0 + jnp.exp(-jnp.abs(x)))


def _rms(x, g):
    return x * lax.rsqrt(jnp.mean(x * x, axis=-1, keepdims=True) + EPS) * g


def _resident(shape):
    return pl.BlockSpec(shape, lambda *_: (0,) * len(shape), pipeline_mode=pl.Buffered(1))


def _inproj_body(x_ref, g_ref, w_ref, o_ref):
    o_ref[...] = _dot(_rms(x_ref[...], g_ref[...]).astype(BF16), w_ref[...])


def _inproj(x2d, g, w, *, tm):
    M, K = x2d.shape
    N = w.shape[1]
    return pl.pallas_call(
        _inproj_body,
        grid=(M // tm,),
        in_specs=[pl.BlockSpec((tm, K), lambda i: (i, 0)), _resident((1, K)), _resident((K, N))],
        out_specs=pl.BlockSpec((tm, N), lambda i: (i, 0)),
        out_shape=jax.ShapeDtypeStruct((M, N), F32),
        compiler_params=_cparams("parallel"),
        name="inproj",
    )(x2d, g, w)


def _merge_body(x_ref, g_ref, wg_ref, ys5_ref, ygdn_ref, yml_ref, w0_ref, w1_ref, w2_ref, wo_ref, o_ref):
    x = x_ref[...]
    xn = _rms(x, g_ref[...]).astype(BF16)
    merged = None
    for k, (y_ref, w_ref) in enumerate(((ys5_ref, w0_ref), (ygdn_ref, w1_ref), (yml_ref, w2_ref))):
        gate = _sigmoid(_dot(xn, wg_ref[:, k * D_MODEL:(k + 1) * D_MODEL]))
        term = gate * _dot(y_ref[...].astype(BF16), w_ref[...])
        merged = term if merged is None else merged + term
    o_ref[...] = x + _dot(merged.astype(BF16), wo_ref[...])


def _merge(x2d, g, wg, ys5, ygdn, yml, w0, w1, w2, wo, *, tm):
    M = x2d.shape[0]
    rows = lambda w: pl.BlockSpec((tm, w), lambda i: (i, 0))
    return pl.pallas_call(
        _merge_body,
        grid=(M // tm,),
        in_specs=[rows(D_MODEL), _resident((1, D_MODEL)), _resident((D_MODEL, 3 * D_MODEL)),
                  rows(512), rows(512), rows(512),
                  _resident((512, D_MODEL)), _resident((512, D_MODEL)), _resident((512, D_MODEL)),
                  _resident((D_MODEL, D_MODEL))],
        out_specs=rows(D_MODEL),
        out_shape=jax.ShapeDtypeStruct((M, D_MODEL), F32),
        compiler_params=_cparams("parallel"),
        name="merge",
    )(x2d, g, wg, ys5, ygdn, yml, w0, w1, w2, wo)


def _mlp_body(x_ref, g_ref, wu_ref, wd_ref, p_ref, wp_ref, wpg_ref, gf_ref, o_ref, xn_ref, acc_ref, *, final):
    j = pl.program_id(1)

    @pl.when(j == 0)
    def _():
        xn_ref[...] = _rms(x_ref[...], g_ref[...]).astype(BF16)
        acc_ref[...] = jnp.zeros_like(acc_ref)

    hdn = jnp.maximum(_dot(xn_ref[...], wu_ref[...]), 0.0)
    acc_ref[...] += _dot((hdn * hdn).astype(BF16), wd_ref[...])

    @pl.when(j == pl.num_programs(1) - 1)
    def _():
        x2 = x_ref[...] + acc_ref[...]
        ple = _dot(p_ref[...].astype(BF16), wp_ref[...]) * _sigmoid(_dot(x2.astype(BF16), wpg_ref[...]))
        x3 = x2 + ple
        o_ref[...] = _rms(x3, gf_ref[...]) if final else x3


def _mlp(x2d, g2, wu, wd, p2d, wp, wpg, gf, *, tm, tf, final):
    M = x2d.shape[0]
    return pl.pallas_call(
        functools.partial(_mlp_body, final=final),
        grid=(M // tm, D_FF // tf),
        in_specs=[pl.BlockSpec((tm, D_MODEL), lambda i, j: (i, 0)),
                  pl.BlockSpec((1, D_MODEL), lambda i, j: (0, 0)),
                  pl.BlockSpec((D_MODEL, tf), lambda i, j: (0, j)),
                  pl.BlockSpec((tf, D_MODEL), lambda i, j: (j, 0)),
                  pl.BlockSpec((tm, PLE_DIM), lambda i, j: (i, 0)),
                  pl.BlockSpec((PLE_DIM, D_MODEL), lambda i, j: (0, 0)),
                  pl.BlockSpec((D_MODEL, D_MODEL), lambda i, j: (0, 0)),
                  pl.BlockSpec((1, D_MODEL), lambda i, j: (0, 0))],
        out_specs=pl.BlockSpec((tm, D_MODEL), lambda i, j: (i, 0)),
        out_shape=jax.ShapeDtypeStruct((M, D_MODEL), F32),
        scratch_shapes=[pltpu.VMEM((tm, D_MODEL), BF16), pltpu.VMEM((tm, D_MODEL), F32)],
        compiler_params=_cparams("parallel", "arbitrary"),
        name="mlp",
    )(x2d, g2, wu, wd, p2d, wp, wpg, gf)


def _s5_body(u_ref, h0r_ref, h0i_ref, lr_ref, li_ref, bbr_ref, bbi_ref, cbr_ref, cbi_ref, d_ref, wg_ref, bg_ref,
             y_ref, hfr_ref, hfi_ref, sr_ref, si_ref, bur_ref, bui_ref, *, rows, steps):
    i = pl.program_id(0)
    tb = rows * steps

    @pl.when(i == 0)
    def _():
        sr_ref[...] = h0r_ref[...]
        si_ref[...] = h0i_ref[...]

    u = u_ref[...].reshape(tb, S5_WIDTH)
    ub = u.astype(BF16)
    if steps > 1:
        r_i = lax.broadcasted_iota(jnp.int32, (tb, tb), 0)
        c_i = lax.broadcasted_iota(jnp.int32, (tb, tb), 1)
        to_tb = (c_i == (r_i % rows) * steps + r_i // rows).astype(BF16)
        to_bt = (r_i == (c_i % rows) * steps + c_i // rows).astype(BF16)
        ub = _dot(to_tb, ub).astype(BF16)
    nb = S5_BLOCK_STATES
    for blk in range(S5_BLOCKS):
        ublk = ub[:, blk * 128:(blk + 1) * 128]
        bur_ref[:, blk * nb:(blk + 1) * nb] = _dot(ublk, bbr_ref[blk])
        bui_ref[:, blk * nb:(blk + 1) * nb] = _dot(ublk, bbi_ref[blk])

    for blk in range(S5_BLOCKS):
        cols = slice(blk * nb, (blk + 1) * nb)
        lr = jnp.broadcast_to(lr_ref[:, cols], (rows, nb))
        li = jnp.broadcast_to(li_ref[:, cols], (rows, nb))

        def step(t, carry, cols=cols, lr=lr, li=li):
            hr, hi = carry
            r = pl.ds(pl.multiple_of(t * rows, rows), rows)
            nr = lr * hr - li * hi + bur_ref[r, cols]
            ni = lr * hi + li * hr + bui_ref[r, cols]
            bur_ref[r, cols] = nr
            bui_ref[r, cols] = ni
            return nr, ni

        carry = (sr_ref[:, cols], si_ref[:, cols])
        if steps == 1:
            carry = step(0, carry)
        else:
            carry = lax.fori_loop(0, steps, step, carry, unroll=8)
        sr_ref[:, cols] = carry[0]
        si_ref[:, cols] = carry[1]

    ys = []
    for blk in range(S5_BLOCKS):
        cols = slice(blk * nb, (blk + 1) * nb)
        ys.append(_dot(bur_ref[:, cols].astype(BF16), cbr_ref[blk])
                  - _dot(bui_ref[:, cols].astype(BF16), cbi_ref[blk]))
    y = jnp.concatenate(ys, axis=1)
    if steps > 1:
        y1 = y.astype(BF16)
        r1 = y - y1.astype(F32)
        y2 = r1.astype(BF16)
        y3 = (r1 - y2.astype(F32)).astype(BF16)
        y = _dot(to_bt, y1) + (_dot(to_bt, y2) + _dot(to_bt, y3))
    y = y + d_ref[...] * u
    y = 0.5 * y * (1.0 + jnp.tanh(math.sqrt(2.0 / math.pi) * (y + 0.044715 * (y * y * y))))
    y = y * _sigmoid(_dot(y.astype(BF16), wg_ref[...]) + bg_ref[...])
    y_ref[...] = y.reshape(y_ref.shape)

    @pl.when(i == pl.num_programs(0) - 1)
    def _():
        hfr_ref[...] = sr_ref[...]
        hfi_ref[...] = si_ref[...]


def _s5(proj, h0r, h0i, prm, *, steps):
    rows = proj.shape[0]
    tb = rows * steps
    cu = COL_U // S5_WIDTH
    if proj.ndim == 3:
        nchunks = proj.shape[1] // steps
        u_spec = pl.BlockSpec((rows, steps, S5_WIDTH), lambda i: (0, i, cu))
        y_spec = pl.BlockSpec((rows, steps, S5_WIDTH), lambda i: (0, i, 0))
        y_shape = (rows, proj.shape[1], S5_WIDTH)
    else:
        assert steps == 1
        nchunks = 1
        u_spec = pl.BlockSpec((rows, S5_WIDTH), lambda i: (0, cu))
        y_spec = pl.BlockSpec((rows, S5_WIDTH), lambda i: (0, 0))
        y_shape = (rows, S5_WIDTH)
    full = lambda shape: pl.BlockSpec(shape, lambda i: (0,) * len(shape))
    return pl.pallas_call(
        functools.partial(_s5_body, rows=rows, steps=steps),
        grid=(nchunks,),
        in_specs=[u_spec,
                  full((rows, S5_NSTATE)), full((rows, S5_NSTATE)),
                  full((1, S5_NSTATE)), full((1, S5_NSTATE)),
                  full((S5_BLOCKS, 128, S5_BLOCK_STATES)), full((S5_BLOCKS, 128, S5_BLOCK_STATES)),
                  full((S5_BLOCKS, S5_BLOCK_STATES, 128)), full((S5_BLOCKS, S5_BLOCK_STATES, 128)),
                  full((1, S5_WIDTH)), full((S5_WIDTH, S5_WIDTH)), full((1, S5_WIDTH))],
        out_specs=[y_spec, full((rows, S5_NSTATE)), full((rows, S5_NSTATE))],
        out_shape=[jax.ShapeDtypeStruct(y_shape, F32),
                   jax.ShapeDtypeStruct((rows, S5_NSTATE), F32),
                   jax.ShapeDtypeStruct((rows, S5_NSTATE), F32)],
        scratch_shapes=[pltpu.VMEM((rows, S5_NSTATE), F32), pltpu.VMEM((rows, S5_NSTATE), F32),
                        pltpu.VMEM((tb, S5_NSTATE), F32), pltpu.VMEM((tb, S5_NSTATE), F32)],
        compiler_params=_cparams("arbitrary"),
        name="s5",
    )(proj, h0r, h0i, prm["lam_re"], prm["lam_im"], prm["bbd_re"], prm["bbd_im"],
      prm["cbd_re"], prm["cbd_im"], prm["d"], prm["w_glu"], prm["b_glu"])


def _s5_params(A_re, A_im, log_dt, B_re, B_im, C_re, C_im, d_skip, w_glu, b_glu):
    dt = jnp.exp(log_dt)[:, None]
    mag = jnp.exp(A_re * dt)
    lam_re, lam_im = mag * jnp.cos(A_im * dt), mag * jnp.sin(A_im * dt)
    inv = 1.0 / (A_re * A_re + A_im * A_im)
    zr, zi = lam_re - 1.0, lam_im
    fac_re = (zr * A_re + zi * A_im) * inv
    fac_im = (zi * A_re - zr * A_im) * inv
    bb_re = fac_re[..., None] * B_re - fac_im[..., None] * B_im
    bb_im = fac_re[..., None] * B_im + fac_im[..., None] * B_re
    gpb = S5_GROUPS // S5_BLOCKS
    eye = jnp.eye(gpb, dtype=F32)

    def in_map(bb):
        t = bb.reshape(S5_BLOCKS, gpb, S5_STATE, S5_GROUP).transpose(0, 1, 3, 2)
        return jnp.einsum('bgcp,gh->bgchp', t, eye).reshape(S5_BLOCKS, 128, S5_BLOCK_STATES).astype(BF16)

    def out_map(cc):
        t = cc.reshape(S5_BLOCKS, gpb, S5_GROUP, S5_STATE).transpose(0, 1, 3, 2)
        return jnp.einsum('bgpc,gh->bgphc', t, eye).reshape(S5_BLOCKS, S5_BLOCK_STATES, 128).astype(BF16)

    return dict(lam_re=lam_re.reshape(1, S5_NSTATE), lam_im=lam_im.reshape(1, S5_NSTATE),
                bbd_re=in_map(bb_re), bbd_im=in_map(bb_im), cbd_re=out_map(C_re), cbd_im=out_map(C_im),
                d=d_skip.reshape(1, S5_WIDTH), w_glu=w_glu.astype(BF16), b_glu=b_glu.reshape(1, S5_WIDTH))


def _tri_masks(c):
    row = lax.broadcasted_iota(jnp.int32, (c, c), 0)
    col = lax.broadcasted_iota(jnp.int32, (c, c), 1)
    return row >= col, row > col


def _head_block_masks(c, heads):
    n = c * heads
    row = lax.broadcasted_iota(jnp.int32, (n, n), 0)
    col = lax.broadcasted_iota(jnp.int32, (n, n), 1)
    same = (row // c) == (col // c)
    return same & (row >= col), same & (row > col)


def _split(a):
    hi = a.astype(BF16)
    return hi, (a - hi.astype(F32)).astype(BF16)


def _gdn_body(qkv_ref, z_ref, sm_ref, cb0_ref, s0_ref, cw_ref, sb_ref, nega_ref, ng_ref,
              o_ref, cbo_ref, so_ref, xs_ref, s_ref, *, c, nc):
    j = pl.program_id(1)
    tb = c * nc

    @pl.when(j == 0)
    def _():
        xs_ref[5:8, :] = cb0_ref[...]
        s_ref[...] = s0_ref[...]

    xs_ref[8:8 + tb, :] = qkv_ref[...]

    incl, _ = _tri_masks(c)
    tri = incl.astype(F32)
    incl_bd, strict_bd = _head_block_masks(c, GDN_HEADS)
    cw = cw_ref[...]
    nsteps = max(1, int(math.ceil(math.log2(c))))

    for ci in range(nc):
        r0 = 8 + ci * c
        conv = xs_ref[r0 - 3:r0 - 3 + c, :] * cw[0:1]
        conv = conv + xs_ref[r0 - 2:r0 - 2 + c, :] * cw[1:2]
        conv = conv + xs_ref[r0 - 1:r0 - 1 + c, :] * cw[2:3]
        conv = conv + xs_ref[r0:r0 + c, :] * cw[3:4]
        conv = conv * _sigmoid(conv)
        rows = slice(ci * c, (ci + 1) * c)
        zz = z_ref[rows, :]
        pre = sm_ref[rows, :] + sb_ref[...]
        beta_all = _sigmoid(pre)
        g_all = nega_ref[...] * _softplus(pre)
        gcum = _dot_hi(tri, g_all)
        gcum_t = gcum.T
        H = GDN_HEADS
        stack = lambda f: jnp.concatenate([f(h) for h in range(H)], axis=0)
        q = stack(lambda h: conv[:, 128 * h:128 * (h + 1)])
        k = stack(lambda h: conv[:, 512 + 128 * h:512 + 128 * (h + 1)])
        v = stack(lambda h: conv[:, 1024 + 128 * h:1024 + 128 * (h + 1)])
        q = q * lax.rsqrt(jnp.sum(q * q, axis=-1, keepdims=True) + EPS) * (GDN_DK ** -0.5)
        k = k * lax.rsqrt(jnp.sum(k * k, axis=-1, keepdims=True) + EPS)
        beta = stack(lambda h: beta_all[:, LANE_BETA + h:LANE_BETA + h + 1])
        gc = stack(lambda h: gcum[:, LANE_G + h:LANE_G + h + 1])
        g_end = stack(lambda h: jnp.broadcast_to(gcum[c - 1:c, LANE_G + h:LANE_G + h + 1], (c, 1)))
        gr = jnp.concatenate([gcum_t[LANE_G + h:LANE_G + h + 1, :] for h in range(H)], axis=1)
        decay = jnp.exp(jnp.where(incl_bd, gc - gr, -jnp.inf))
        kb = k * beta
        a_low = jnp.where(strict_bd, _dot_nt(kb, k) * decay, 0.0)
        eg = jnp.exp(gc)
        x = jnp.concatenate([v * beta, kb * eg], axis=1)
        m = -a_low
        for it in range(nsteps):
            mb = m.astype(BF16)
            xh, xl = _split(x)
            x = x + (_dot(mb, xh) + _dot(mb, xl))
            if it < nsteps - 1:
                m = _dot(mb, mb)
        uu, ww = x[:, :GDN_DV], x[:, GDN_DV:]
        attn = jnp.where(incl_bd, _dot_nt(q, k) * decay, 0.0)
        q_dec = q * eg
        k_dec = k * jnp.exp(g_end - gc)
        hrows = lambda a, h: a[c * h:c * (h + 1)]
        s_old = [s_ref[h] for h in range(H)]
        v_new = uu - stack(lambda h: _dot(hrows(ww, h), s_old[h]))
        o = stack(lambda h: _dot(hrows(q_dec, h), s_old[h])) + _dot(attn, v_new)
        s_scale = jnp.exp(g_end)
        for h in range(H):
            s_ref[h] = s_old[h] * s_scale[c * h:c * h + 1] + _dot_tn(hrows(k_dec, h), hrows(v_new, h))
        o = o * lax.rsqrt(jnp.mean(o * o, axis=-1, keepdims=True) + EPS) * ng_ref[...]
        zs = stack(lambda h: zz[:, 128 * h:128 * (h + 1)])
        o = o * (zs * _sigmoid(zs))
        for h in range(H):
            o_ref[rows, 128 * h:128 * (h + 1)] = o[c * h:c * (h + 1)]

    tail = xs_ref[8 + tb - 3:8 + tb, :]
    xs_ref[5:8, :] = tail
    cbo_ref[...] = tail

    @pl.when(j == pl.num_programs(1) - 1)
    def _():
        so_ref[...] = s_ref[...]


def _gdn(proj3, cb0, s0, prm, *, c, nc):
    B, Lr, _ = proj3.shape
    valid = c * nc
    nblk = Lr // valid
    cq, cz, cs = COL_QKV // 1536, COL_Z // 512, COL_SMALL // 128
    return pl.pallas_call(
        functools.partial(_gdn_body, c=c, nc=nc),
        grid=(B, nblk),
        in_specs=[pl.BlockSpec((None, valid, 1536), lambda b, j: (b, j, cq)),
                  pl.BlockSpec((None, valid, 512), lambda b, j: (b, j, cz)),
                  pl.BlockSpec((None, valid, 128), lambda b, j: (b, j, cs)),
                  pl.BlockSpec((None, 3, GDN_CONV_CH), lambda b, j: (b, 0, 0)),
                  pl.BlockSpec((None, GDN_HEADS, GDN_DK, GDN_DV), lambda b, j: (b, 0, 0, 0)),
                  pl.BlockSpec((GDN_CONV_W, GDN_CONV_CH), lambda b, j: (0, 0)),
                  pl.BlockSpec((1, 128), lambda b, j: (0, 0)),
                  pl.BlockSpec((1, 128), lambda b, j: (0, 0)),
                  pl.BlockSpec((1, 128), lambda b, j: (0, 0))],
        out_specs=[pl.BlockSpec((None, valid, 512), lambda b, j: (b, j, 0)),
                   pl.BlockSpec((None, 3, GDN_CONV_CH), lambda b, j: (b, 0, 0)),
                   pl.BlockSpec((None, GDN_HEADS, GDN_DK, GDN_DV), lambda b, j: (b, 0, 0, 0))],
        out_shape=[jax.ShapeDtypeStruct((B, Lr, 512), F32),
                   jax.ShapeDtypeStruct((B, 3, GDN_CONV_CH), F32),
                   jax.ShapeDtypeStruct((B, GDN_HEADS, GDN_DK, GDN_DV), F32)],
        scratch_shapes=[pltpu.VMEM((8 + c * nc, GDN_CONV_CH), F32),
                        pltpu.VMEM((GDN_HEADS, GDN_DK, GDN_DV), F32)],
        compiler_params=_cparams("parallel", "arbitrary"),
        name="gdn",
    )(proj3, proj3, proj3, cb0, s0, prm["conv_w"], prm["bias"], prm["nega"], prm["norm_g"])


def _gdn_step_body(qkv_ref, z_ref, sm_ref, cb_ref, s_ref, cw_ref, sb_ref, nega_ref, ng_ref,
                   o_ref, cbo_ref, so_ref, *, nb):
    x = qkv_ref[...]
    cw = cw_ref[...]
    b0, b1, b2 = cb_ref[0], cb_ref[1], cb_ref[2]
    conv = ((b0 * cw[0:1] + b1 * cw[1:2]) + b2 * cw[2:3]) + x * cw[3:4]
    cbo_ref[0] = b1
    cbo_ref[1] = b2
    cbo_ref[2] = x
    conv = conv * _sigmoid(conv)
    pre = sm_ref[...] + sb_ref[...]
    beta_all = _sigmoid(pre)
    eg_all = jnp.exp(nega_ref[...] * _softplus(pre))
    zz = z_ref[...]
    for h in range(GDN_HEADS):
        q = conv[:, 128 * h:128 * (h + 1)]
        k = conv[:, 512 + 128 * h:512 + 128 * (h + 1)]
        v = conv[:, 1024 + 128 * h:1024 + 128 * (h + 1)]
        q = q * lax.rsqrt(jnp.sum(q * q, axis=-1, keepdims=True) + EPS) * (GDN_DK ** -0.5)
        k = k * lax.rsqrt(jnp.sum(k * k, axis=-1, keepdims=True) + EPS)
        beta = beta_all[:, LANE_BETA + h:LANE_BETA + h + 1]
        eg = eg_all[:, LANE_G + h:LANE_G + h + 1]
        kb = k * beta
        uu = v * beta
        attn = jnp.sum(q * k, axis=-1, keepdims=True)
        w_t = (kb * eg).T
        q_t = (q * eg).T
        k_t = k.T
        rows = []
        for b in range(nb):
            s_old = s_ref[b, h]
            ws = jnp.sum(s_old * w_t[:, b:b + 1], axis=0, keepdims=True)
            qs = jnp.sum(s_old * q_t[:, b:b + 1], axis=0, keepdims=True)
            v_new = uu[b:b + 1] - ws
            rows.append(qs + attn[b:b + 1] * v_new)
            so_ref[b, h] = s_old * eg[b:b + 1] + k_t[:, b:b + 1] * v_new
        o = jnp.concatenate(rows, axis=0)
        o = o * lax.rsqrt(jnp.mean(o * o, axis=-1, keepdims=True) + EPS) * ng_ref[...]
        zh = zz[:, 128 * h:128 * (h + 1)]
        o_ref[:, 128 * h:128 * (h + 1)] = o * (zh * _sigmoid(zh))


def _gdn_step(proj, cb0, s0, prm, *, nb):
    B = proj.shape[0]
    cq, cz, cs = COL_QKV // 1536, COL_Z // 512, COL_SMALL // 128
    vec = pl.BlockSpec((1, 128), lambda i: (0, 0))
    o, cbn, sn = pl.pallas_call(
        functools.partial(_gdn_step_body, nb=nb),
        grid=(B // nb,),
        in_specs=[pl.BlockSpec((nb, 1536), lambda i: (i, cq)),
                  pl.BlockSpec((nb, 512), lambda i: (i, cz)),
                  pl.BlockSpec((nb, 128), lambda i: (i, cs)),
                  pl.BlockSpec((3, nb, GDN_CONV_CH), lambda i: (0, i, 0)),
                  pl.BlockSpec((nb, GDN_HEADS, GDN_DK, GDN_DV), lambda i: (i, 0, 0, 0)),
                  pl.BlockSpec((GDN_CONV_W, GDN_CONV_CH), lambda i: (0, 0)), vec, vec, vec],
        out_specs=[pl.BlockSpec((nb, 512), lambda i: (i, 0)),
                   pl.BlockSpec((3, nb, GDN_CONV_CH), lambda i: (0, i, 0)),
                   pl.BlockSpec((nb, GDN_HEADS, GDN_DK, GDN_DV), lambda i: (i, 0, 0, 0))],
        out_shape=[jax.ShapeDtypeStruct((B, 512), F32),
                   jax.ShapeDtypeStruct((3, B, GDN_CONV_CH), F32),
                   jax.ShapeDtypeStruct((B, GDN_HEADS, GDN_DK, GDN_DV), F32)],
        compiler_params=_cparams("parallel"),
        name="gdn_step",
    )(proj, proj, proj, jnp.swapaxes(cb0, 0, 1), s0, prm["conv_w"], prm["bias"], prm["nega"], prm["norm_g"])
    return o, jnp.swapaxes(cbn, 0, 1), sn


def _mlstm_body(q_ref, k_ref, v_ref, og_ref, sm_ref, c0_ref, n0_ref, m0_ref, sb_ref, ng_ref,
                o_ref, co_ref, no_ref, mo_ref, c_ref, n_ref, m_ref, *, c, nc):
    j = pl.program_id(1)
    H = ML_HEADS

    @pl.when(j == 0)
    def _():
        c_ref[...] = c0_ref[...]
        n_ref[...] = n0_ref[...]
        m_ref[...] = m0_ref[...]

    incl, _ = _tri_masks(c)
    tri = incl.astype(F32)
    srow_t = lax.broadcasted_iota(jnp.int32, (H * c, c), 0) % c
    scol = lax.broadcasted_iota(jnp.int32, (H * c, c), 1)
    incl_st = srow_t >= scol
    lane = lax.broadcasted_iota(jnp.int32, (1, ML_QK_W), 1)
    srow = lax.broadcasted_iota(jnp.int32, (ML_QK_W, 1), 0)
    hmask = [((lane >= ML_DK * h) & (lane < ML_DK * (h + 1))).astype(F32) for h in range(H)]
    rmask = [((srow >= ML_DK * h) & (srow < ML_DK * (h + 1))).astype(F32) for h in range(H)]
    stack = lambda f: jnp.concatenate([f(h) for h in range(H)], axis=0)
    col = lambda a: jnp.broadcast_to(a, (c, 1))

    for ci in range(nc):
        rows = slice(ci * c, (ci + 1) * c)
        q, v, og = q_ref[rows, :], v_ref[rows, :], og_ref[rows, :]
        k = k_ref[rows, :] * (ML_DK ** -0.5)
        pre = sm_ref[rows, :] + sb_ref[...]
        li_all = GATE_CAP * jnp.tanh(pre / GATE_CAP)
        lf_all = -_softplus(-li_all)
        bcum = _dot_hi(tri, lf_all)
        bcum_t = bcum.T
        li_t = li_all.T
        c_old = c_ref[...]
        n_old = n_ref[...]
        bc = stack(lambda h: bcum[:, LANE_F + h:LANE_F + h + 1])
        lic = stack(lambda h: li_all[:, LANE_I + h:LANE_I + h + 1])
        bc_end = stack(lambda h: col(bcum[c - 1:c, LANE_F + h:LANE_F + h + 1]))
        m_prev = stack(lambda h: col(m_ref[h:h + 1, 0:1]))
        rowterm = stack(lambda h: jnp.broadcast_to(
            li_t[LANE_I + h:LANE_I + h + 1, :] - bcum_t[LANE_F + h:LANE_F + h + 1, :], (c, c)))
        logw = jnp.where(incl_st, bc + rowterm, -jnp.inf)
        inter = bc + m_prev
        m_t = jnp.maximum(inter, jnp.max(logw, axis=1, keepdims=True))
        wts = jnp.exp(logw - m_t)
        sc = jnp.exp(inter - m_t)
        qs = stack(lambda h: q * hmask[h])
        s_qk = _dot_nt(qs, k) * wts
        sv = _dot(s_qk, v)
        num = stack(lambda h: sv[c * h:c * (h + 1), ML_DV * h:ML_DV * (h + 1)]) + sc * _dot(qs, c_old)
        den = jnp.sum(s_qk, axis=1, keepdims=True) + sc * jnp.sum(qs * n_old, axis=1, keepdims=True)
        hh = num / jnp.maximum(jnp.abs(den), jnp.exp(-m_t))
        m_end = [m_t[c * h + c - 1:c * h + c, :] for h in range(H)]
        sc_end = [sc[c * h + c - 1:c * h + c, :] for h in range(H)]
        w_last = jnp.exp(bc_end - bc + lic - stack(lambda h: col(m_end[h])))
        kw = stack(lambda h: k * hmask[h]) * w_last
        vs = stack(lambda h: v[:, ML_DV * h:ML_DV * (h + 1)])
        sc_rows = sum(sc_end[h] * rmask[h] for h in range(H))
        sc_lanes = sum(sc_end[h] * hmask[h] for h in range(H))
        c_ref[...] = sc_rows * c_old + _dot_tn(kw, vs)
        n_ref[...] = sc_lanes * n_old + jnp.sum(kw, axis=0, keepdims=True)
        hh = hh * lax.rsqrt(jnp.mean(hh * hh, axis=-1, keepdims=True) + EPS)
        for h in range(H):
            m_ref[h:h + 1, :] = jnp.broadcast_to(m_end[h], (1, 128))
            hs = slice(ML_DV * h, ML_DV * (h + 1))
            o_ref[rows, hs] = hh[c * h:c * (h + 1)] * ng_ref[:, hs] * _sigmoid(og[:, hs])

    @pl.when(j == pl.num_programs(1) - 1)
    def _():
        co_ref[...] = c_ref[...]
        no_ref[...] = n_ref[...]
        mo_ref[...] = m_ref[...]


def _mlstm(proj3, c0, n0, m0, prm, *, c, nc):
    B, Lr, _ = proj3.shape
    valid = c * nc
    nblk = Lr // valid
    cq, ck, cv, co, cs = COL_MQ // 256, COL_MK // 256, COL_MV // 512, COL_MO // 512, COL_SMALL // 128
    st = lambda shape: pl.BlockSpec((None,) + shape, lambda b, j: (b,) + (0,) * len(shape))
    return pl.pallas_call(
        functools.partial(_mlstm_body, c=c, nc=nc),
        grid=(B, nblk),
        in_specs=[pl.BlockSpec((None, valid, 256), lambda b, j: (b, j, cq)),
                  pl.BlockSpec((None, valid, 256), lambda b, j: (b, j, ck)),
                  pl.BlockSpec((None, valid, 512), lambda b, j: (b, j, cv)),
                  pl.BlockSpec((None, valid, 512), lambda b, j: (b, j, co)),
                  pl.BlockSpec((None, valid, 128), lambda b, j: (b, j, cs)),
                  st((ML_QK_W, ML_DV)), st((1, ML_QK_W)), st((8, 128)),
                  pl.BlockSpec((1, 128), lambda b, j: (0, 0)),
                  pl.BlockSpec((1, 512), lambda b, j: (0, 0))],
        out_specs=[pl.BlockSpec((None, valid, 512), lambda b, j: (b, j, 0)),
                   st((ML_QK_W, ML_DV)), st((1, ML_QK_W)), st((8, 128))],
        out_shape=[jax.ShapeDtypeStruct((B, Lr, 512), F32),
                   jax.ShapeDtypeStruct((B, ML_QK_W, ML_DV), F32),
                   jax.ShapeDtypeStruct((B, 1, ML_QK_W), F32),
                   jax.ShapeDtypeStruct((B, 8, 128), F32)],
        scratch_shapes=[pltpu.VMEM((ML_QK_W, ML_DV), F32), pltpu.VMEM((1, ML_QK_W), F32),
                        pltpu.VMEM((8, 128), F32)],
        compiler_params=_cparams("parallel", "arbitrary"),
        name="mlstm",
    )(proj3, proj3, proj3, proj3, proj3, c0, n0, m0, prm["bias"], prm["norm_g"])


def _mlstm_step_body(q_ref, k_ref, v_ref, og_ref, sm_ref, c_ref, n_ref, m_ref, sb_ref, ng_ref,
                     o_ref, co_ref, no_ref, mo_ref, *, nb):
    q = q_ref[...]
    k = k_ref[...] * (ML_DK ** -0.5)
    v = v_ref[...]
    og = og_ref[...]
    n_old = n_ref[...]
    m_old = m_ref[...]
    pre = sm_ref[...] + sb_ref[...]
    capped = GATE_CAP * jnp.tanh(pre / GATE_CAP)
    lf_all = -_softplus(-capped)
    lane = lax.broadcasted_iota(jnp.int32, (1, ML_QK_W), 1)
    lane128 = lax.broadcasted_iota(jnp.int32, (1, 128), 1)
    q_t = q.T
    k_t = k.T
    n_new = jnp.zeros((nb, ML_QK_W), F32)
    m_new = jnp.zeros((nb, 128), F32)
    for h in range(ML_HEADS):
        hmask = ((lane >= ML_DK * h) & (lane < ML_DK * (h + 1))).astype(F32)
        li = capped[:, LANE_I + h:LANE_I + h + 1]
        inter = lf_all[:, LANE_F + h:LANE_F + h + 1] + m_old[:, h:h + 1]
        m_t = jnp.maximum(inter, li)
        wts = jnp.exp(li - m_t)
        sc = jnp.exp(inter - m_t)
        qh = q * hmask
        s_qk = jnp.sum(qh * k, axis=-1, keepdims=True) * wts
        den = s_qk + sc * jnp.sum(qh * n_old, axis=-1, keepdims=True)
        vh = v[:, ML_DV * h:ML_DV * (h + 1)]
        hs = slice(ML_DK * h, ML_DK * (h + 1))
        rows = []
        for b in range(nb):
            c_old = c_ref[b, hs, :]
            rows.append(jnp.sum(c_old * q_t[hs, b:b + 1], axis=0, keepdims=True))
            co_ref[b, hs, :] = sc[b:b + 1] * c_old + (wts[b:b + 1] * k_t[hs, b:b + 1]) * vh[b:b + 1]
        num = s_qk * vh + sc * jnp.concatenate(rows, axis=0)
        hh = num / jnp.maximum(jnp.abs(den), jnp.exp(-m_t))
        hh = hh * lax.rsqrt(jnp.mean(hh * hh, axis=-1, keepdims=True) + EPS) * ng_ref[:, ML_DV * h:ML_DV * (h + 1)]
        o_ref[:, ML_DV * h:ML_DV * (h + 1)] = hh * _sigmoid(og[:, ML_DV * h:ML_DV * (h + 1)])
        n_new = n_new + hmask * (sc * n_old + wts * k)
        m_new = m_new + m_t * (lane128 == h).astype(F32)
    no_ref[...] = n_new
    mo_ref[...] = m_new


def _mlstm_step(proj, c0, n0, m0, prm, *, nb):
    B = proj.shape[0]
    cq, ck, cv, co, cs = COL_MQ // 256, COL_MK // 256, COL_MV // 512, COL_MO // 512, COL_SMALL // 128
    return pl.pallas_call(
        functools.partial(_mlstm_step_body, nb=nb),
        grid=(B // nb,),
        in_specs=[pl.BlockSpec((nb, 256), lambda i: (i, cq)),
                  pl.BlockSpec((nb, 256), lambda i: (i, ck)),
                  pl.BlockSpec((nb, 512), lambda i: (i, cv)),
                  pl.BlockSpec((nb, 512), lambda i: (i, co)),
                  pl.BlockSpec((nb, 128), lambda i: (i, cs)),
                  pl.BlockSpec((nb, ML_QK_W, ML_DV), lambda i: (i, 0, 0)),
                  pl.BlockSpec((nb, ML_QK_W), lambda i: (i, 0)),
                  pl.BlockSpec((nb, 128), lambda i: (i, 0)),
                  pl.BlockSpec((1, 128), lambda i: (0, 0)),
                  pl.BlockSpec((1, 512), lambda i: (0, 0))],
        out_specs=[pl.BlockSpec((nb, 512), lambda i: (i, 0)),
                   pl.BlockSpec((nb, ML_QK_W, ML_DV), lambda i: (i, 0, 0)),
                   pl.BlockSpec((nb, ML_QK_W), lambda i: (i, 0)),
                   pl.BlockSpec((nb, 128), lambda i: (i, 0))],
        out_shape=[jax.ShapeDtypeStruct((B, 512), F32),
                   jax.ShapeDtypeStruct((B, ML_QK_W, ML_DV), F32),
                   jax.ShapeDtypeStruct((B, ML_QK_W), F32),
                   jax.ShapeDtypeStruct((B, 128), F32)],
        compiler_params=_cparams("parallel"),
        name="mlstm_step",
    )(proj, proj, proj, proj, proj, c0, n0, m0, prm["bias"], prm["norm_g"])


def _lane_vec(pieces):
    v = jnp.zeros((128,), F32)
    for off, val in pieces:
        v = v.at[off:off + val.shape[0]].set(val.astype(F32))
    return v.reshape(1, 128)


def _layer_params(i, w):
    win = w["w_in"][i].astype(BF16)
    o = [0, 512, 2048, 2560, 2564, 2568, 2824, 3080, 3592, 4104, 4108, 4112, 7184]
    seg = lambda a: win[:, o[a]:o[a + 1]]
    w_u, w_qkv, w_z, w_b, w_a, w_mq, w_mk, w_mv, w_mo, w_mi, w_mf, w_gate = (seg(a) for a in range(12))
    small = jnp.concatenate([w_b, w_a, w_mi, w_mf, jnp.zeros((D_MODEL, 128 - 16), BF16)], axis=1)
    w_main = jnp.concatenate([w_qkv, w_z, w_mv, w_mo, w_u, w_mq, w_mk, small], axis=1)
    bias = _lane_vec([(LANE_G, w["gdn_dt_bias"][i]), (LANE_I, w["ml_b_i"][i]), (LANE_F, w["ml_b_f"][i])])
    return dict(
        norm1_g=w["norm1_g"][i].reshape(1, D_MODEL), w_main=w_main, w_gate=w_gate,
        s5=_s5_params(w["s5_A_re"][i], w["s5_A_im"][i], w["s5_log_dt"][i], w["s5_B_re"][i], w["s5_B_im"][i],
                      w["s5_C_re"][i], w["s5_C_im"][i], w["s5_D"][i], w["s5_w_glu"][i], w["s5_b_glu"][i]),
        gdn=dict(conv_w=w["gdn_conv_w"][i], bias=bias, nega=_lane_vec([(LANE_G, -jnp.exp(w["gdn_A_log"][i]))]),
                 norm_g=w["gdn_norm_g"][i].reshape(1, GDN_DV)),
        ml=dict(bias=bias, norm_g=w["ml_norm_g"][i].reshape(1, 512)),
        w_br_s5=w["w_br_s5"][i].astype(BF16), w_br_gdn=w["w_br_gdn"][i].astype(BF16),
        w_br_ml=w["w_br_ml"][i].astype(BF16), w_out=w["w_out"][i].astype(BF16),
        norm2_g=w["norm2_g"][i].reshape(1, D_MODEL), w_up=w["w_up"][i].astype(BF16),
        w_down=w["w_down"][i].astype(BF16), w_ple=w["w_ple"][i].astype(BF16),
        w_ple_gate=w["w_ple_gate"][i].astype(BF16))


def _block(x2d, p2d, B, L, state, lp, gf, final):
    s5r, s5i, cbuf, gS, mC, mn, mm = state
    prompt = L > 1
    tm, tmm, tml, tf = (512, 512, 1024, 1024) if prompt else (B, B, B, 1024)
    proj = _inproj(x2d, lp["norm1_g"], lp["w_main"], tm=tm)
    h0 = (s5r.reshape(B, S5_NSTATE), s5i.reshape(B, S5_NSTATE))
    if prompt:
        proj3 = proj.reshape(B, L, PROJ_W)
        y_s5, h_re, h_im = _s5(proj3, *h0, lp["s5"], steps=CHUNK)
        y_gdn, cbuf_n, gS_n = _gdn(proj3, cbuf, gS, lp["gdn"], c=CHUNK, nc=MIXER_CHUNKS_PER_STEP)
        m_in = jnp.broadcast_to(jnp.pad(mm, ((0, 0), (0, 8 - ML_HEADS)))[:, :, None], (B, 8, 128))
        y_ml, mC_n, mn_n, mm_n = _mlstm(proj3, mC.reshape(B, ML_QK_W, ML_DV), mn.reshape(B, 1, ML_QK_W), m_in,
                                        lp["ml"], c=CHUNK, nc=MIXER_CHUNKS_PER_STEP)
        mm_n = mm_n[:, :ML_HEADS, 0]
    else:
        y_s5, h_re, h_im = _s5(proj, *h0, lp["s5"], steps=1)
        y_gdn, cbuf_n, gS_n = _gdn_step(proj, cbuf, gS, lp["gdn"], nb=STEP_ROWS)
        y_ml, mC_n, mn_n, mm_n = _mlstm_step(proj, mC.reshape(B, ML_QK_W, ML_DV), mn.reshape(B, ML_QK_W),
                                             jnp.pad(mm, ((0, 0), (0, 128 - ML_HEADS))), lp["ml"], nb=STEP_ROWS)
        mm_n = mm_n[:, :ML_HEADS]
    x2d = _merge(x2d, lp["norm1_g"], lp["w_gate"], y_s5.reshape(B * L, 512), y_gdn.reshape(B * L, 512),
                 y_ml.reshape(B * L, 512), lp["w_br_s5"], lp["w_br_gdn"], lp["w_br_ml"], lp["w_out"], tm=tmm)
    x2d = _mlp(x2d, lp["norm2_g"], lp["w_up"], lp["w_down"], p2d, lp["w_ple"], lp["w_ple_gate"], gf,
               tm=tml, tf=tf, final=final)
    new_state = (h_re.reshape(B, S5_GROUPS, S5_STATE), h_im.reshape(B, S5_GROUPS, S5_STATE), cbuf_n, gS_n,
                 mC_n.reshape(B, ML_HEADS, ML_DK, ML_DV), mn_n.reshape(B, ML_HEADS, ML_DK), mm_n)
    return x2d, new_state


def kernel(x_prompt, x_sample, state_s5_re, state_s5_im, state_gdn_conv, state_gdn, state_mlstm_C, state_mlstm_n, state_mlstm_m, p_prompt, p_sample, norm1_g, w_in, s5_A_re, s5_A_im, s5_log_dt, s5_B_re, s5_B_im, s5_C_re, s5_C_im, s5_D, s5_w_glu, s5_b_glu, gdn_conv_w, gdn_A_log, gdn_dt_bias, gdn_norm_g, ml_b_i, ml_b_f, ml_norm_g, w_br_s5, w_br_gdn, w_br_ml, w_out, norm2_g, w_up, w_down, w_ple, w_ple_gate, final_norm_g):
    w = dict(norm1_g=norm1_g, w_in=w_in, s5_A_re=s5_A_re, s5_A_im=s5_A_im, s5_log_dt=s5_log_dt, s5_B_re=s5_B_re,
             s5_B_im=s5_B_im, s5_C_re=s5_C_re, s5_C_im=s5_C_im, s5_D=s5_D, s5_w_glu=s5_w_glu, s5_b_glu=s5_b_glu,
             gdn_conv_w=gdn_conv_w, gdn_A_log=gdn_A_log, gdn_dt_bias=gdn_dt_bias, gdn_norm_g=gdn_norm_g,
             ml_b_i=ml_b_i, ml_b_f=ml_b_f, ml_norm_g=ml_norm_g, w_br_s5=w_br_s5, w_br_gdn=w_br_gdn,
             w_br_ml=w_br_ml, w_out=w_out, norm2_g=norm2_g, w_up=w_up, w_down=w_down, w_ple=w_ple,
             w_ple_gate=w_ple_gate)
    bp, lp_len, _ = x_prompt.shape
    bs, ls_len, _ = x_sample.shape
    gf = final_norm_g.reshape(1, D_MODEL)
    xp = x_prompt.reshape(bp * lp_len, D_MODEL)
    xs = x_sample.reshape(bs * ls_len, D_MODEL)
    new_p, new_s = [], []
    for i in range(DEPTH):
        lp = _layer_params(i, w)
        zero_state = (jnp.zeros((bp, S5_GROUPS, S5_STATE), F32), jnp.zeros((bp, S5_GROUPS, S5_STATE), F32),
                      jnp.zeros((bp, GDN_CONV_W - 1, GDN_CONV_CH), F32),
                      jnp.zeros((bp, GDN_HEADS, GDN_DK, GDN_DV), F32),
                      jnp.zeros((bp, ML_HEADS, ML_DK, ML_DV), F32), jnp.zeros((bp, ML_HEADS, ML_DK), F32),
                      jnp.zeros((bp, ML_HEADS), F32))
        final = i == DEPTH - 1
        xp, sp_i = _block(xp, p_prompt[i].reshape(bp * lp_len, PLE_DIM), bp, lp_len, zero_state, lp, gf, final)
        sample_state = (state_s5_re[i], state_s5_im[i], state_gdn_conv[i], state_gdn[i], state_mlstm_C[i],
                        state_mlstm_n[i], state_mlstm_m[i])
        xs, ss_i = _block(xs, p_sample[i].reshape(bs * ls_len, PLE_DIM), bs, ls_len, sample_state, lp, gf, final)
        new_p.append(sp_i)
        new_s.append(ss_i)
    sp = [jnp.stack(t) for t in zip(*new_p)]
    ss = [jnp.stack(t) for t in zip(*new_s)]
    return (xp.reshape(bp, lp_len, D_MODEL), xs.reshape(bs, ls_len, D_MODEL),
            sp[0], ss[0], sp[1], ss[1], sp[2], ss[2], sp[3], ss[3], sp[4], ss[4], sp[5], ss[5], sp[6], ss[6])
```

```python
import functools
import math

import jax
import jax.numpy as jnp
from jax import lax
from jax.experimental import pallas as pl
from jax.experimental.pallas import tpu as pltpu

F32 = jnp.float32
BF16 = jnp.bfloat16
HI = lax.Precision.HIGHEST

D_MODEL = 1024
DEPTH = 2
PLE_DIM = 256
D_FF = 4 * D_MODEL
EPS = 1e-6
CHUNK = 64

S5_WIDTH = 512
S5_GROUP = 16
S5_GROUPS = 32
S5_STATE = 64
S5_NSTATE = S5_GROUPS * S5_STATE
S5_BLOCKS = 4
S5_BLOCK_STATES = S5_NSTATE // S5_BLOCKS

GDN_HEADS = 4
GDN_DK = 128
GDN_DV = 128
GDN_CONV_W = 4
GDN_CONV_CH = 1536

ML_HEADS = 4
ML_DK = 64
ML_DV = 128
ML_QK_W = ML_HEADS * ML_DK
GATE_CAP = 15.0

PROJ_W = 4224
COL_QKV = 0
COL_Z = 1536
COL_MV = 2048
COL_MO = 2560
COL_U = 3072
COL_MQ = 3584
COL_MK = 3840
COL_SMALL = 4096
LANE_BETA, LANE_G, LANE_I, LANE_F = 0, 4, 8, 12

STEP_ROWS = 8
MIXER_CHUNKS_PER_STEP = 4

VMEM_LIMIT = 52 * 1024 * 1024


def _cparams(*sem):
    return pltpu.CompilerParams(dimension_semantics=sem, vmem_limit_bytes=VMEM_LIMIT)


def _bf(a):
    return a if a.dtype == BF16 else a.astype(BF16)


def _dot(a, b):
    return jnp.dot(_bf(a), _bf(b), preferred_element_type=F32)


def _dot_hi(a, b):
    return jnp.dot(a, b, preferred_element_type=F32, precision=HI)


def _dot_nt(a, b):
    return lax.dot_general(_bf(a), _bf(b), (((1,), (1,)), ((), ())), preferred_element_type=F32)


def _dot_tn(a, b):
    return lax.dot_general(_bf(a), _bf(b), (((0,), (0,)), ((), ())), preferred_element_type=F32)


def _sigmoid(x):
    return 1.0 / (1.0 + jnp.exp(-x))


def _softplus(x):
    return jnp.maximum(x, 0.0) + jnp.log(1.0 + jnp.exp(-jnp.abs(x)))


def _rms(x, g):
    return x * lax.rsqrt(jnp.mean(x * x, axis=-1, keepdims=True) + EPS) * g


def _resident(shape):
    return pl.BlockSpec(shape, lambda *_: (0,) * len(shape), pipeline_mode=pl.Buffered(1))


def _inproj_body(x_ref, g_ref, w_ref, o_ref):
    o_ref[...] = _dot(_rms(x_ref[...], g_ref[...]).astype(BF16), w_ref[...])


def _inproj(x2d, g, w, *, tm):
    M, K = x2d.shape
    N = w.shape[1]
    return pl.pallas_call(
        _inproj_body,
        grid=(M // tm,),
        in_specs=[pl.BlockSpec((tm, K), lambda i: (i, 0)), _resident((1, K)), _resident((K, N))],
        out_specs=pl.BlockSpec((tm, N), lambda i: (i, 0)),
        out_shape=jax.ShapeDtypeStruct((M, N), F32),
        compiler_params=_cparams("parallel"),
        name="inproj",
    )(x2d, g, w)


def _merge_body(x_ref, g_ref, wg_ref, ys5_ref, ygdn_ref, yml_ref, w0_ref, w1_ref, w2_ref, wo_ref, o_ref):
    x = x_ref[...]
    xn = _rms(x, g_ref[...]).astype(BF16)
    merged = None
    for k, (y_ref, w_ref) in enumerate(((ys5_ref, w0_ref), (ygdn_ref, w1_ref), (yml_ref, w2_ref))):
        gate = _sigmoid(_dot(xn, wg_ref[:, k * D_MODEL:(k + 1) * D_MODEL]))
        term = gate * _dot(y_ref[...].astype(BF16), w_ref[...])
        merged = term if merged is None else merged + term
    o_ref[...] = x + _dot(merged.astype(BF16), wo_ref[...])


def _merge(x2d, g, wg, ys5, ygdn, yml, w0, w1, w2, wo, *, tm):
    M = x2d.shape[0]
    rows = lambda w: pl.BlockSpec((tm, w), lambda i: (i, 0))
    return pl.pallas_call(
        _merge_body,
        grid=(M // tm,),
        in_specs=[rows(D_MODEL), _resident((1, D_MODEL)), _resident((D_MODEL, 3 * D_MODEL)),
                  rows(512), rows(512), rows(512),
                  _resident((512, D_MODEL)), _resident((512, D_MODEL)), _resident((512, D_MODEL)),
                  _resident((D_MODEL, D_MODEL))],
        out_specs=rows(D_MODEL),
        out_shape=jax.ShapeDtypeStruct((M, D_MODEL), F32),
        compiler_params=_cparams("parallel"),
        name="merge",
    )(x2d, g, wg, ys5, ygdn, yml, w0, w1, w2, wo)


def _mlp_body(x_ref, g_ref, wu_ref, wd_ref, p_ref, wp_ref, wpg_ref, gf_ref, o_ref, xn_ref, acc_ref, *, final):
    j = pl.program_id(1)

    @pl.when(j == 0)
    def _():
        xn_ref[...] = _rms(x_ref[...], g_ref[...]).astype(BF16)
        acc_ref[...] = jnp.zeros_like(acc_ref)

    hdn = jnp.maximum(_dot(xn_ref[...], wu_ref[...]), 0.0)
    acc_ref[...] += _dot((hdn * hdn).astype(BF16), wd_ref[...])

    @pl.when(j == pl.num_programs(1) - 1)
    def _():
        x2 = x_ref[...] + acc_ref[...]
        ple = _dot(p_ref[...].astype(BF16), wp_ref[...]) * _sigmoid(_dot(x2.astype(BF16), wpg_ref[...]))
        x3 = x2 + ple
        o_ref[...] = _rms(x3, gf_ref[...]) if final else x3


def _mlp(x2d, g2, wu, wd, p2d, wp, wpg, gf, *, tm, tf, final):
    M = x2d.shape[0]
    return pl.pallas_call(
        functools.partial(_mlp_body, final=final),
        grid=(M // tm, D_FF // tf),
        in_specs=[pl.BlockSpec((tm, D_MODEL), lambda i, j: (i, 0)),
                  pl.BlockSpec((1, D_MODEL), lambda i, j: (0, 0)),
                  pl.BlockSpec((D_MODEL, tf), lambda i, j: (0, j)),
                  pl.BlockSpec((tf, D_MODEL), lambda i, j: (j, 0)),
                  pl.BlockSpec((tm, PLE_DIM), lambda i, j: (i, 0)),
                  pl.BlockSpec((PLE_DIM, D_MODEL), lambda i, j: (0, 0)),
                  pl.BlockSpec((D_MODEL, D_MODEL), lambda i, j: (0, 0)),
                  pl.BlockSpec((1, D_MODEL), lambda i, j: (0, 0))],
        out_specs=pl.BlockSpec((tm, D_MODEL), lambda i, j: (i, 0)),
        out_shape=jax.ShapeDtypeStruct((M, D_MODEL), F32),
        scratch_shapes=[pltpu.VMEM((tm, D_MODEL), BF16), pltpu.VMEM((tm, D_MODEL), F32)],
        compiler_params=_cparams("parallel", "arbitrary"),
        name="mlp",
    )(x2d, g2, wu, wd, p2d, wp, wpg, gf)


def _s5_body(u_ref, h0r_ref, h0i_ref, lr_ref, li_ref, bbr_ref, bbi_ref, cbr_ref, cbi_ref, d_ref, wg_ref, bg_ref,
             y_ref, hfr_ref, hfi_ref, sr_ref, si_ref, bur_ref, bui_ref, *, rows, steps):
    i = pl.program_id(0)
    tb = rows * steps

    @pl.when(i == 0)
    def _():
        sr_ref[...] = h0r_ref[...]
        si_ref[...] = h0i_ref[...]

    u = u_ref[...].reshape(tb, S5_WIDTH)
    ub = u.astype(BF16)
    if steps > 1:
        r_i = lax.broadcasted_iota(jnp.int32, (tb, tb), 0)
        c_i = lax.broadcasted_iota(jnp.int32, (tb, tb), 1)
        to_tb = (c_i == (r_i % rows) * steps + r_i // rows).astype(BF16)
        to_bt = (r_i == (c_i % rows) * steps + c_i // rows).astype(BF16)
        ub = _dot(to_tb, ub).astype(BF16)
    nb = S5_BLOCK_STATES
    for blk in range(S5_BLOCKS):
        ublk = ub[:, blk * 128:(blk + 1) * 128]
        bur_ref[:, blk * nb:(blk + 1) * nb] = _dot(ublk, bbr_ref[blk])
        bui_ref[:, blk * nb:(blk + 1) * nb] = _dot(ublk, bbi_ref[blk])

    for blk in range(S5_BLOCKS):
        cols = slice(blk * nb, (blk + 1) * nb)
        lr = jnp.broadcast_to(lr_ref[:, cols], (rows, nb))
        li = jnp.broadcast_to(li_ref[:, cols], (rows, nb))

        def step(t, carry, cols=cols, lr=lr, li=li):
            hr, hi = carry
            r = pl.ds(pl.multiple_of(t * rows, rows), rows)
            nr = lr * hr - li * hi + bur_ref[r, cols]
            ni = lr * hi + li * hr + bui_ref[r, cols]
            bur_ref[r, cols] = nr
            bui_ref[r, cols] = ni
            return nr, ni

        carry = (sr_ref[:, cols], si_ref[:, cols])
        if steps == 1:
            carry = step(0, carry)
        else:
            carry = lax.fori_loop(0, steps, step, carry, unroll=8)
        sr_ref[:, cols] = carry[0]
        si_ref[:, cols] = carry[1]

    ys = []
    for blk in range(S5_BLOCKS):
        cols = slice(blk * nb, (blk + 1) * nb)
        ys.append(_dot(bur_ref[:, cols].astype(BF16), cbr_ref[blk])
                  - _dot(bui_ref[:, cols].astype(BF16), cbi_ref[blk]))
    y = jnp.concatenate(ys, axis=1)
    if steps > 1:
        y1 = y.astype(BF16)
        r1 = y - y1.astype(F32)
        y2 = r1.astype(BF16)
        y3 = (r1 - y2.astype(F32)).astype(BF16)
        y = _dot(to_bt, y1) + (_dot(to_bt, y2) + _dot(to_bt, y3))
    y = y + d_ref[...] * u
    y = 0.5 * y * (1.0 + jnp.tanh(math.sqrt(2.0 / math.pi) * (y + 0.044715 * (y * y * y))))
    y = y * _sigmoid(_dot(y.astype(BF16), wg_ref[...]) + bg_ref[...])
    y_ref[...] = y.reshape(y_ref.shape)

    @pl.when(i == pl.num_programs(0) - 1)
    def _():
        hfr_ref[...] = sr_ref[...]
        hfi_ref[...] = si_ref[...]


def _s5(proj, h0r, h0i, prm, *, steps):
    rows = proj.shape[0]
    tb = rows * steps
    cu = COL_U // S5_WIDTH
    if proj.ndim == 3:
        nchunks = proj.shape[1] // steps
        u_spec = pl.BlockSpec((rows, steps, S5_WIDTH), lambda i: (0, i, cu))
        y_spec = pl.BlockSpec((rows, steps, S5_WIDTH), lambda i: (0, i, 0))
        y_shape = (rows, proj.shape[1], S5_WIDTH)
    else:
        assert steps == 1
        nchunks = 1
        u_spec = pl.BlockSpec((rows, S5_WIDTH), lambda i: (0, cu))
        y_spec = pl.BlockSpec((rows, S5_WIDTH), lambda i: (0, 0))
        y_shape = (rows, S5_WIDTH)
    full = lambda shape: pl.BlockSpec(shape, lambda i: (0,) * len(shape))
    return pl.pallas_call(
        functools.partial(_s5_body, rows=rows, steps=steps),
        grid=(nchunks,),
        in_specs=[u_spec,
                  full((rows, S5_NSTATE)), full((rows, S5_NSTATE)),
                  full((1, S5_NSTATE)), full((1, S5_NSTATE)),
                  full((S5_BLOCKS, 128, S5_BLOCK_STATES)), full((S5_BLOCKS, 128, S5_BLOCK_STATES)),
                  full((S5_BLOCKS, S5_BLOCK_STATES, 128)), full((S5_BLOCKS, S5_BLOCK_STATES, 128)),
                  full((1, S5_WIDTH)), full((S5_WIDTH, S5_WIDTH)), full((1, S5_WIDTH))],
        out_specs=[y_spec, full((rows, S5_NSTATE)), full((rows, S5_NSTATE))],
        out_shape=[jax.ShapeDtypeStruct(y_shape, F32),
                   jax.ShapeDtypeStruct((rows, S5_NSTATE), F32),
                   jax.ShapeDtypeStruct((rows, S5_NSTATE), F32)],
        scratch_shapes=[pltpu.VMEM((rows, S5_NSTATE), F32), pltpu.VMEM((rows, S5_NSTATE), F32),
                        pltpu.VMEM((tb, S5_NSTATE), F32), pltpu.VMEM((tb, S5_NSTATE), F32)],
        compiler_params=_cparams("arbitrary"),
        name="s5",
    )(proj, h0r, h0i, prm["lam_re"], prm["lam_im"], prm["bbd_re"], prm["bbd_im"],
      prm["cbd_re"], prm["cbd_im"], prm["d"], prm["w_glu"], prm["b_glu"])


def _s5_params(A_re, A_im, log_dt, B_re, B_im, C_re, C_im, d_skip, w_glu, b_glu):
    dt = jnp.exp(log_dt)[:, None]
    mag = jnp.exp(A_re * dt)
    lam_re, lam_im = mag * jnp.cos(A_im * dt), mag * jnp.sin(A_im * dt)
    inv = 1.0 / (A_re * A_re + A_im * A_im)
    zr, zi = lam_re - 1.0, lam_im
    fac_re = (zr * A_re + zi * A_im) * inv
    fac_im = (zi * A_re - zr * A_im) * inv
    bb_re = fac_re[..., None] * B_re - fac_im[..., None] * B_im
    bb_im = fac_re[..., None] * B_im + fac_im[..., None] * B_re
    gpb = S5_GROUPS // S5_BLOCKS
    eye = jnp.eye(gpb, dtype=F32)

    def in_map(bb):
        t = bb.reshape(S5_BLOCKS, gpb, S5_STATE, S5_GROUP).transpose(0, 1, 3, 2)
        return jnp.einsum('bgcp,gh->bgchp', t, eye).reshape(S5_BLOCKS, 128, S5_BLOCK_STATES).astype(BF16)

    def out_map(cc):
        t = cc.reshape(S5_BLOCKS, gpb, S5_GROUP, S5_STATE).transpose(0, 1, 3, 2)
        return jnp.einsum('bgpc,gh->bgphc', t, eye).reshape(S5_BLOCKS, S5_BLOCK_STATES, 128).astype(BF16)

    return dict(lam_re=lam_re.reshape(1, S5_NSTATE), lam_im=lam_im.reshape(1, S5_NSTATE),
                bbd_re=in_map(bb_re), bbd_im=in_map(bb_im), cbd_re=out_map(C_re), cbd_im=out_map(C_im),
                d=d_skip.reshape(1, S5_WIDTH), w_glu=w_glu.astype(BF16), b_glu=b_glu.reshape(1, S5_WIDTH))


def _tri_masks(c):
    row = lax.broadcasted_iota(jnp.int32, (c, c), 0)
    col = lax.broadcasted_iota(jnp.int32, (c, c), 1)
    return row >= col, row > col


def _head_block_masks(c, heads):
    n = c * heads
    row = lax.broadcasted_iota(jnp.int32, (n, n), 0)
    col = lax.broadcasted_iota(jnp.int32, (n, n), 1)
    levels = [((row >> (l + 1)) == (col >> (l + 1))) & (((row >> l) & 1) == 1) & (((col >> l) & 1) == 0)
              for l in range(c.bit_length() - 1)]
    return ((row // c) == (col // c)) & (row >= col), levels


def _split(a):
    hi = a.astype(BF16)
    return hi, (a - hi.astype(F32)).astype(BF16)


def _gdn_body(qkv_ref, z_ref, sm_ref, cb0_ref, s0_ref, cw_ref, sb_ref, nega_ref, ng_ref,
              o_ref, cbo_ref, so_ref, xs_ref, s_ref, *, c, nc):
    j = pl.program_id(1)
    tb = c * nc

    @pl.when(j == 0)
    def _():
        xs_ref[5:8, :] = cb0_ref[...]
        s_ref[...] = s0_ref[...]

    xs_ref[8:8 + tb, :] = qkv_ref[...]

    incl, _ = _tri_masks(c)
    tri = incl.astype(F32)
    incl_bd, level_masks = _head_block_masks(c, GDN_HEADS)
    nlevels = len(level_masks)
    cw = cw_ref[...]

    H = GDN_HEADS
    stack = lambda f: jnp.concatenate([f(h) for h in range(H)], axis=0)
    hrows = lambda a, h: a[c * h:c * (h + 1)]

    chunks = []
    for ci in range(nc):
        r0 = 8 + ci * c
        conv = xs_ref[r0 - 3:r0 - 3 + c, :] * cw[0:1]
        conv = conv + xs_ref[r0 - 2:r0 - 2 + c, :] * cw[1:2]
        conv = conv + xs_ref[r0 - 1:r0 - 1 + c, :] * cw[2:3]
        conv = conv + xs_ref[r0:r0 + c, :] * cw[3:4]
        conv = conv * _sigmoid(conv)
        rows = slice(ci * c, (ci + 1) * c)
        pre = sm_ref[rows, :] + sb_ref[...]
        beta_all = _sigmoid(pre)
        g_all = nega_ref[...] * _softplus(pre)
        gcum = _dot_hi(tri, g_all)
        gcum_t = gcum.T
        q = stack(lambda h: conv[:, 128 * h:128 * (h + 1)])
        k = stack(lambda h: conv[:, 512 + 128 * h:512 + 128 * (h + 1)])
        v = stack(lambda h: conv[:, 1024 + 128 * h:1024 + 128 * (h + 1)])
        q = q * lax.rsqrt(jnp.sum(q * q, axis=-1, keepdims=True) + EPS) * (GDN_DK ** -0.5)
        k = k * lax.rsqrt(jnp.sum(k * k, axis=-1, keepdims=True) + EPS)
        beta = stack(lambda h: beta_all[:, LANE_BETA + h:LANE_BETA + h + 1])
        gc = stack(lambda h: gcum[:, LANE_G + h:LANE_G + h + 1])
        g_end = stack(lambda h: jnp.broadcast_to(gcum[c - 1:c, LANE_G + h:LANE_G + h + 1], (c, 1)))
        gr = jnp.concatenate([gcum_t[LANE_G + h:LANE_G + h + 1, :] for h in range(H)], axis=1)
        decay = jnp.exp(jnp.where(incl_bd, gc - gr, -jnp.inf))
        kb = k * beta
        eg = jnp.exp(gc)
        chunks.append(dict(
            rows=rows, q=q, k=k, decay=decay, eg=eg, gc=gc, g_end=g_end,
            a=_dot_nt(kb, k) * decay,
            rhs=jnp.concatenate([v * beta, kb * eg], axis=1)))

    for lvl in range(nlevels):
        mask = level_masks[lvl]
        for d in chunks:
            am = jnp.where(mask, d["a"], 0.0)
            if lvl == 0:
                d["n"] = -am
            else:
                nb = d["n"].astype(BF16)
                w = am + _dot(am.astype(BF16), nb)
                d["n"] = d["n"] - (w + _dot(nb, w.astype(BF16)))
    for d in chunks:
        nb = d["n"].astype(BF16)
        rh, rl = _split(d["rhs"])
        d["x"] = d["rhs"] + (_dot(nb, rh) + _dot(nb, rl))

    for d in chunks:
        d["attn"] = jnp.where(incl_bd, _dot_nt(d["q"], d["k"]) * d["decay"], 0.0)
        d["q_dec"] = d["q"] * d["eg"]
        d["k_dec"] = d["k"] * jnp.exp(d["g_end"] - d["gc"])
        d["s_scale"] = jnp.exp(d["g_end"])

    for d in chunks:
        uu, ww = d["x"][:, :GDN_DV], d["x"][:, GDN_DV:]
        s_old = [s_ref[h] for h in range(H)]
        v_new = uu - stack(lambda h: _dot(hrows(ww, h), s_old[h]))
        for h in range(H):
            s_ref[h] = (s_old[h] * d["s_scale"][c * h:c * h + 1]
                        + _dot_tn(hrows(d["k_dec"], h), hrows(v_new, h)))
        o = stack(lambda h: _dot(hrows(d["q_dec"], h), s_old[h])) + _dot(d["attn"], v_new)
        o = o * lax.rsqrt(jnp.mean(o * o, axis=-1, keepdims=True) + EPS) * ng_ref[...]
        zz = z_ref[d["rows"], :]
        zs = stack(lambda h: zz[:, 128 * h:128 * (h + 1)])
        o = o * (zs * _sigmoid(zs))
        for h in range(H):
            o_ref[d["rows"], 128 * h:128 * (h + 1)] = o[c * h:c * (h + 1)]

    tail = xs_ref[8 + tb - 3:8 + tb, :]
    xs_ref[5:8, :] = tail
    cbo_ref[...] = tail

    @pl.when(j == pl.num_programs(1) - 1)
    def _():
        so_ref[...] = s_ref[...]


def _gdn(proj3, cb0, s0, prm, *, c, nc):
    B, Lr, _ = proj3.shape
    valid = c * nc
    nblk = Lr // valid
    cq, cz, cs = COL_QKV // 1536, COL_Z // 512, COL_SMALL // 128
    return pl.pallas_call(
        functools.partial(_gdn_body, c=c, nc=nc),
        grid=(B, nblk),
        in_specs=[pl.BlockSpec((None, valid, 1536), lambda b, j: (b, j, cq)),
                  pl.BlockSpec((None, valid, 512), lambda b, j: (b, j, cz)),
                  pl.BlockSpec((None, valid, 128), lambda b, j: (b, j, cs)),
                  pl.BlockSpec((None, 3, GDN_CONV_CH), lambda b, j: (b, 0, 0)),
                  pl.BlockSpec((None, GDN_HEADS, GDN_DK, GDN_DV), lambda b, j: (b, 0, 0, 0)),
                  pl.BlockSpec((GDN_CONV_W, GDN_CONV_CH), lambda b, j: (0, 0)),
                  pl.BlockSpec((1, 128), lambda b, j: (0, 0)),
                  pl.BlockSpec((1, 128), lambda b, j: (0, 0)),
                  pl.BlockSpec((1, 128), lambda b, j: (0, 0))],
        out_specs=[pl.BlockSpec((None, valid, 512), lambda b, j: (b, j, 0)),
                   pl.BlockSpec((None, 3, GDN_CONV_CH), lambda b, j: (b, 0, 0)),
                   pl.BlockSpec((None, GDN_HEADS, GDN_DK, GDN_DV), lambda b, j: (b, 0, 0, 0))],
        out_shape=[jax.ShapeDtypeStruct((B, Lr, 512), F32),
                   jax.ShapeDtypeStruct((B, 3, GDN_CONV_CH), F32),
                   jax.ShapeDtypeStruct((B, GDN_HEADS, GDN_DK, GDN_DV), F32)],
        scratch_shapes=[pltpu.VMEM((8 + c * nc, GDN_CONV_CH), F32),
                        pltpu.VMEM((GDN_HEADS, GDN_DK, GDN_DV), F32)],
        compiler_params=_cparams("parallel", "arbitrary"),
        name="gdn",
    )(proj3, proj3, proj3, cb0, s0, prm["conv_w"], prm["bias"], prm["nega"], prm["norm_g"])


def _gdn_step_body(qkv_ref, z_ref, sm_ref, cb_ref, s_ref, cw_ref, sb_ref, nega_ref, ng_ref,
                   o_ref, cbo_ref, so_ref, *, nb):
    x = qkv_ref[...]
    cw = cw_ref[...]
    b0, b1, b2 = cb_ref[0], cb_ref[1], cb_ref[2]
    conv = ((b0 * cw[0:1] + b1 * cw[1:2]) + b2 * cw[2:3]) + x * cw[3:4]
    cbo_ref[0] = b1
    cbo_ref[1] = b2
    cbo_ref[2] = x
    conv = conv * _sigmoid(conv)
    pre = sm_ref[...] + sb_ref[...]
    beta_all = _sigmoid(pre)
    eg_all = jnp.exp(nega_ref[...] * _softplus(pre))
    zz = z_ref[...]
    for h in range(GDN_HEADS):
        q = conv[:, 128 * h:128 * (h + 1)]
        k = conv[:, 512 + 128 * h:512 + 128 * (h + 1)]
        v = conv[:, 1024 + 128 * h:1024 + 128 * (h + 1)]
        q = q * lax.rsqrt(jnp.sum(q * q, axis=-1, keepdims=True) + EPS) * (GDN_DK ** -0.5)
        k = k * lax.rsqrt(jnp.sum(k * k, axis=-1, keepdims=True) + EPS)
        beta = beta_all[:, LANE_BETA + h:LANE_BETA + h + 1]
        eg = eg_all[:, LANE_G + h:LANE_G + h + 1]
        kb = k * beta
        uu = v * beta
        attn = jnp.sum(q * k, axis=-1, keepdims=True)
        w_t = (kb * eg).T
        q_t = (q * eg).T
        k_t = k.T
        rows = []
        for b in range(nb):
            s_old = s_ref[b, h]
            ws = jnp.sum(s_old * w_t[:, b:b + 1], axis=0, keepdims=True)
            qs = jnp.sum(s_old * q_t[:, b:b + 1], axis=0, keepdims=True)
            v_new = uu[b:b + 1] - ws
            rows.append(qs + attn[b:b + 1] * v_new)
            so_ref[b, h] = s_old * eg[b:b + 1] + k_t[:, b:b + 1] * v_new
        o = jnp.concatenate(rows, axis=0)
        o = o * lax.rsqrt(jnp.mean(o * o, axis=-1, keepdims=True) + EPS) * ng_ref[...]
        zh = zz[:, 128 * h:128 * (h + 1)]
        o_ref[:, 128 * h:128 * (h + 1)] = o * (zh * _sigmoid(zh))


def _gdn_step(proj, cb0, s0, prm, *, nb):
    B = proj.shape[0]
    cq, cz, cs = COL_QKV // 1536, COL_Z // 512, COL_SMALL // 128
    vec = pl.BlockSpec((1, 128), lambda i: (0, 0))
    o, cbn, sn = pl.pallas_call(
        functools.partial(_gdn_step_body, nb=nb),
        grid=(B // nb,),
        in_specs=[pl.BlockSpec((nb, 1536), lambda i: (i, cq)),
                  pl.BlockSpec((nb, 512), lambda i: (i, cz)),
                  pl.BlockSpec((nb, 128), lambda i: (i, cs)),
                  pl.BlockSpec((3, nb, GDN_CONV_CH), lambda i: (0, i, 0)),
                  pl.BlockSpec((nb, GDN_HEADS, GDN_DK, GDN_DV), lambda i: (i, 0, 0, 0)),
                  pl.BlockSpec((GDN_CONV_W, GDN_CONV_CH), lambda i: (0, 0)), vec, vec, vec],
        out_specs=[pl.BlockSpec((nb, 512), lambda i: (i, 0)),
                   pl.BlockSpec((3, nb, GDN_CONV_CH), lambda i: (0, i, 0)),
                   pl.BlockSpec((nb, GDN_HEADS, GDN_DK, GDN_DV), lambda i: (i, 0, 0, 0))],
        out_shape=[jax.ShapeDtypeStruct((B, 512), F32),
                   jax.ShapeDtypeStruct((3, B, GDN_CONV_CH), F32),
                   jax.ShapeDtypeStruct((B, GDN_HEADS, GDN_DK, GDN_DV), F32)],
        compiler_params=_cparams("parallel"),
        name="gdn_step",
    )(proj, proj, proj, jnp.swapaxes(cb0, 0, 1), s0, prm["conv_w"], prm["bias"], prm["nega"], prm["norm_g"])
    return o, jnp.swapaxes(cbn, 0, 1), sn


def _mlstm_body(q_ref, k_ref, v_ref, og_ref, sm_ref, c0_ref, n0_ref, m0_ref, sb_ref, ng_ref,
                o_ref, co_ref, no_ref, mo_ref, c_ref, n_ref, m_ref, *, c, nc):
    j = pl.program_id(1)
    H = ML_HEADS

    @pl.when(j == 0)
    def _():
        c_ref[...] = c0_ref[...]
        n_ref[...] = n0_ref[...]
        m_ref[...] = m0_ref[...]

    incl, _ = _tri_masks(c)
    tri = incl.astype(F32)
    srow_t = lax.broadcasted_iota(jnp.int32, (H * c, c), 0) % c
    scol = lax.broadcasted_iota(jnp.int32, (H * c, c), 1)
    incl_st = srow_t >= scol
    lane = lax.broadcasted_iota(jnp.int32, (1, ML_QK_W), 1)
    srow = lax.broadcasted_iota(jnp.int32, (ML_QK_W, 1), 0)
    hmask = [((lane >= ML_DK * h) & (lane < ML_DK * (h + 1))).astype(F32) for h in range(H)]
    rmask = [((srow >= ML_DK * h) & (srow < ML_DK * (h + 1))).astype(F32) for h in range(H)]
    stack = lambda f: jnp.concatenate([f(h) for h in range(H)], axis=0)
    col = lambda a: jnp.broadcast_to(a, (c, 1))

    chunks = []
    for ci in range(nc):
        rows = slice(ci * c, (ci + 1) * c)
        q, v = q_ref[rows, :], v_ref[rows, :]
        k = k_ref[rows, :] * (ML_DK ** -0.5)
        pre = sm_ref[rows, :] + sb_ref[...]
        li_all = GATE_CAP * jnp.tanh(pre / GATE_CAP)
        lf_all = -_softplus(-li_all)
        bcum = _dot_hi(tri, lf_all)
        bcum_t = bcum.T
        li_t = li_all.T
        bc = stack(lambda h: bcum[:, LANE_F + h:LANE_F + h + 1])
        lic = stack(lambda h: li_all[:, LANE_I + h:LANE_I + h + 1])
        bc_end = stack(lambda h: col(bcum[c - 1:c, LANE_F + h:LANE_F + h + 1]))
        rowterm = stack(lambda h: jnp.broadcast_to(
            li_t[LANE_I + h:LANE_I + h + 1, :] - bcum_t[LANE_F + h:LANE_F + h + 1, :], (c, c)))
        logw = jnp.where(incl_st, bc + rowterm, -jnp.inf)
        qs = stack(lambda h: q * hmask[h])
        chunks.append(dict(rows=rows, k=k, v=v, bc=bc, lic=lic, bc_end=bc_end, logw=logw, qs=qs,
                           rowmax=jnp.max(logw, axis=1, keepdims=True), qk=_dot_nt(qs, k)))

    m_prev = stack(lambda h: col(m_ref[h:h + 1, 0:1]))
    for d in chunks:
        inter = d["bc"] + m_prev
        m_t = jnp.maximum(inter, d["rowmax"])
        d["m_t"], d["sc"] = m_t, jnp.exp(inter - m_t)
        d["m_end"] = [m_t[c * h + c - 1:c * h + c, :] for h in range(H)]
        m_prev = stack(lambda h: col(d["m_end"][h]))
    for h in range(H):
        m_ref[h:h + 1, :] = jnp.broadcast_to(chunks[-1]["m_end"][h], (1, 128))

    for d in chunks:
        k, v, sc = d["k"], d["v"], d["sc"]
        d["s_qk"] = d["qk"] * jnp.exp(d["logw"] - d["m_t"])
        sv = _dot(d["s_qk"], v)
        d["intra"] = stack(lambda h: sv[c * h:c * (h + 1), ML_DV * h:ML_DV * (h + 1)])
        w_last = jnp.exp(d["bc_end"] - d["bc"] + d["lic"] - stack(lambda h: col(d["m_end"][h])))
        kw = stack(lambda h: k * hmask[h]) * w_last
        vs = stack(lambda h: v[:, ML_DV * h:ML_DV * (h + 1)])
        sc_end = [sc[c * h + c - 1:c * h + c, :] for h in range(H)]
        d["sc_rows"] = sum(sc_end[h] * rmask[h] for h in range(H))
        d["sc_lanes"] = sum(sc_end[h] * hmask[h] for h in range(H))
        d["c_inc"] = _dot_tn(kw, vs)
        d["n_inc"] = jnp.sum(kw, axis=0, keepdims=True)

    c_cur, n_cur = c_ref[...], n_ref[...]
    for d in chunks:
        d["c_old"], d["n_old"] = c_cur, n_cur
        c_cur = d["sc_rows"] * c_cur + d["c_inc"]
        n_cur = d["sc_lanes"] * n_cur + d["n_inc"]
    c_ref[...] = c_cur
    n_ref[...] = n_cur

    for d in chunks:
        qs, sc, rows = d["qs"], d["sc"], d["rows"]
        num = d["intra"] + sc * _dot(qs, d["c_old"])
        den = jnp.sum(d["s_qk"], axis=1, keepdims=True) + sc * jnp.sum(qs * d["n_old"], axis=1, keepdims=True)
        hh = num / jnp.maximum(jnp.abs(den), jnp.exp(-d["m_t"]))
        hh = hh * lax.rsqrt(jnp.mean(hh * hh, axis=-1, keepdims=True) + EPS)
        og = og_ref[rows, :]
        for h in range(H):
            hs = slice(ML_DV * h, ML_DV * (h + 1))
            o_ref[rows, hs] = hh[c * h:c * (h + 1)] * ng_ref[:, hs] * _sigmoid(og[:, hs])

    @pl.when(j == pl.num_programs(1) - 1)
    def _():
        co_ref[...] = c_ref[...]
        no_ref[...] = n_ref[...]
        mo_ref[...] = m_ref[...]


def _mlstm(proj3, c0, n0, m0, prm, *, c, nc):
    B, Lr, _ = proj3.shape
    valid = c * nc
    nblk = Lr // valid
    cq, ck, cv, co, cs = COL_MQ // 256, COL_MK // 256, COL_MV // 512, COL_MO // 512, COL_SMALL // 128
    st = lambda shape: pl.BlockSpec((None,) + shape, lambda b, j: (b,) + (0,) * len(shape))
    return pl.pallas_call(
        functools.partial(_mlstm_body, c=c, nc=nc),
        grid=(B, nblk),
        in_specs=[pl.BlockSpec((None, valid, 256), lambda b, j: (b, j, cq)),
                  pl.BlockSpec((None, valid, 256), lambda b, j: (b, j, ck)),
                  pl.BlockSpec((None, valid, 512), lambda b, j: (b, j, cv)),
                  pl.BlockSpec((None, valid, 512), lambda b, j: (b, j, co)),
                  pl.BlockSpec((None, valid, 128), lambda b, j: (b, j, cs)),
                  st((ML_QK_W, ML_DV)), st((1, ML_QK_W)), st((8, 128)),
                  pl.BlockSpec((1, 128), lambda b, j: (0, 0)),
                  pl.BlockSpec((1, 512), lambda b, j: (0, 0))],
        out_specs=[pl.BlockSpec((None, valid, 512), lambda b, j: (b, j, 0)),
                   st((ML_QK_W, ML_DV)), st((1, ML_QK_W)), st((8, 128))],
        out_shape=[jax.ShapeDtypeStruct((B, Lr, 512), F32),
                   jax.ShapeDtypeStruct((B, ML_QK_W, ML_DV), F32),
                   jax.ShapeDtypeStruct((B, 1, ML_QK_W), F32),
                   jax.ShapeDtypeStruct((B, 8, 128), F32)],
        scratch_shapes=[pltpu.VMEM((ML_QK_W, ML_DV), F32), pltpu.VMEM((1, ML_QK_W), F32),
                        pltpu.VMEM((8, 128), F32)],
        compiler_params=_cparams("parallel", "arbitrary"),
        name="mlstm",
    )(proj3, proj3, proj3, proj3, proj3, c0, n0, m0, prm["bias"], prm["norm_g"])


def _mlstm_step_body(q_ref, k_ref, v_ref, og_ref, sm_ref, c_ref, n_ref, m_ref, sb_ref, ng_ref,
                     o_ref, co_ref, no_ref, mo_ref, *, nb):
    q = q_ref[...]
    k = k_ref[...] * (ML_DK ** -0.5)
    v = v_ref[...]
    og = og_ref[...]
    n_old = n_ref[...]
    m_old = m_ref[...]
    pre = sm_ref[...] + sb_ref[...]
    capped = GATE_CAP * jnp.tanh(pre / GATE_CAP)
    lf_all = -_softplus(-capped)
    lane = lax.broadcasted_iota(jnp.int32, (1, ML_QK_W), 1)
    lane128 = lax.broadcasted_iota(jnp.int32, (1, 128), 1)
    q_t = q.T
    k_t = k.T
    n_new = jnp.zeros((nb, ML_QK_W), F32)
    m_new = jnp.zeros((nb, 128), F32)
    for h in range(ML_HEADS):
        hmask = ((lane >= ML_DK * h) & (lane < ML_DK * (h + 1))).astype(F32)
        li = capped[:, LANE_I + h:LANE_I + h + 1]
        inter = lf_all[:, LANE_F + h:LANE_F + h + 1] + m_old[:, h:h + 1]
        m_t = jnp.maximum(inter, li)
        wts = jnp.exp(li - m_t)
        sc = jnp.exp(inter - m_t)
        qh = q * hmask
        s_qk = jnp.sum(qh * k, axis=-1, keepdims=True) * wts
        den = s_qk + sc * jnp.sum(qh * n_old, axis=-1, keepdims=True)
        vh = v[:, ML_DV * h:ML_DV * (h + 1)]
        hs = slice(ML_DK * h, ML_DK * (h + 1))
        rows = []
        for b in range(nb):
            c_old = c_ref[b, hs, :]
            rows.append(jnp.sum(c_old * q_t[hs, b:b + 1], axis=0, keepdims=True))
            co_ref[b, hs, :] = sc[b:b + 1] * c_old + (wts[b:b + 1] * k_t[hs, b:b + 1]) * vh[b:b + 1]
        num = s_qk * vh + sc * jnp.concatenate(rows, axis=0)
        hh = num / jnp.maximum(jnp.abs(den), jnp.exp(-m_t))
        hh = hh * lax.rsqrt(jnp.mean(hh * hh, axis=-1, keepdims=True) + EPS) * ng_ref[:, ML_DV * h:ML_DV * (h + 1)]
        o_ref[:, ML_DV * h:ML_DV * (h + 1)] = hh * _sigmoid(og[:, ML_DV * h:ML_DV * (h + 1)])
        n_new = n_new + hmask * (sc * n_old + wts * k)
        m_new = m_new + m_t * (lane128 == h).astype(F32)
    no_ref[...] = n_new
    mo_ref[...] = m_new


def _mlstm_step(proj, c0, n0, m0, prm, *, nb):
    B = proj.shape[0]
    cq, ck, cv, co, cs = COL_MQ // 256, COL_MK // 256, COL_MV // 512, COL_MO // 512, COL_SMALL // 128
    return pl.pallas_call(
        functools.partial(_mlstm_step_body, nb=nb),
        grid=(B // nb,),
        in_specs=[pl.BlockSpec((nb, 256), lambda i: (i, cq)),
                  pl.BlockSpec((nb, 256), lambda i: (i, ck)),
                  pl.BlockSpec((nb, 512), lambda i: (i, cv)),
                  pl.BlockSpec((nb, 512), lambda i: (i, co)),
                  pl.BlockSpec((nb, 128), lambda i: (i, cs)),
                  pl.BlockSpec((nb, ML_QK_W, ML_DV), lambda i: (i, 0, 0)),
                  pl.BlockSpec((nb, ML_QK_W), lambda i: (i, 0)),
                  pl.BlockSpec((nb, 128), lambda i: (i, 0)),
                  pl.BlockSpec((1, 128), lambda i: (0, 0)),
                  pl.BlockSpec((1, 512), lambda i: (0, 0))],
        out_specs=[pl.BlockSpec((nb, 512), lambda i: (i, 0)),
                   pl.BlockSpec((nb, ML_QK_W, ML_DV), lambda i: (i, 0, 0)),
                   pl.BlockSpec((nb, ML_QK_W), lambda i: (i, 0)),
                   pl.BlockSpec((nb, 128), lambda i: (i, 0))],
        out_shape=[jax.ShapeDtypeStruct((B, 512), F32),
                   jax.ShapeDtypeStruct((B, ML_QK_W, ML_DV), F32),
                   jax.ShapeDtypeStruct((B, ML_QK_W), F32),
                   jax.ShapeDtypeStruct((B, 128), F32)],
        compiler_params=_cparams("parallel"),
        name="mlstm_step",
    )(proj, proj, proj, proj, proj, c0, n0, m0, prm["bias"], prm["norm_g"])


def _lane_vec(pieces):
    v = jnp.zeros((128,), F32)
    for off, val in pieces:
        v = v.at[off:off + val.shape[0]].set(val.astype(F32))
    return v.reshape(1, 128)


def _layer_params(i, w):
    win = w["w_in"][i].astype(BF16)
    o = [0, 512, 2048, 2560, 2564, 2568, 2824, 3080, 3592, 4104, 4108, 4112, 7184]
    seg = lambda a: win[:, o[a]:o[a + 1]]
    w_u, w_qkv, w_z, w_b, w_a, w_mq, w_mk, w_mv, w_mo, w_mi, w_mf, w_gate = (seg(a) for a in range(12))
    small = jnp.concatenate([w_b, w_a, w_mi, w_mf, jnp.zeros((D_MODEL, 128 - 16), BF16)], axis=1)
    w_main = jnp.concatenate([w_qkv, w_z, w_mv, w_mo, w_u, w_mq, w_mk, small], axis=1)
    bias = _lane_vec([(LANE_G, w["gdn_dt_bias"][i]), (LANE_I, w["ml_b_i"][i]), (LANE_F, w["ml_b_f"][i])])
    return dict(
        norm1_g=w["norm1_g"][i].reshape(1, D_MODEL), w_main=w_main, w_gate=w_gate,
        s5=_s5_params(w["s5_A_re"][i], w["s5_A_im"][i], w["s5_log_dt"][i], w["s5_B_re"][i], w["s5_B_im"][i],
                      w["s5_C_re"][i], w["s5_C_im"][i], w["s5_D"][i], w["s5_w_glu"][i], w["s5_b_glu"][i]),
        gdn=dict(conv_w=w["gdn_conv_w"][i], bias=bias, nega=_lane_vec([(LANE_G, -jnp.exp(w["gdn_A_log"][i]))]),
                 norm_g=w["gdn_norm_g"][i].reshape(1, GDN_DV)),
        ml=dict(bias=bias, norm_g=w["ml_norm_g"][i].reshape(1, 512)),
        w_br_s5=w["w_br_s5"][i].astype(BF16), w_br_gdn=w["w_br_gdn"][i].astype(BF16),
        w_br_ml=w["w_br_ml"][i].astype(BF16), w_out=w["w_out"][i].astype(BF16),
        norm2_g=w["norm2_g"][i].reshape(1, D_MODEL), w_up=w["w_up"][i].astype(BF16),
        w_down=w["w_down"][i].astype(BF16), w_ple=w["w_ple"][i].astype(BF16),
        w_ple_gate=w["w_ple_gate"][i].astype(BF16))


def _block(x2d, p2d, B, L, state, lp, gf, final):
    s5r, s5i, cbuf, gS, mC, mn, mm = state
    prompt = L > 1
    tm, tmm, tml, tf = (512, 512, 1024, 1024) if prompt else (B, B, B, 1024)
    proj = _inproj(x2d, lp["norm1_g"], lp["w_main"], tm=tm)
    h0 = (s5r.reshape(B, S5_NSTATE), s5i.reshape(B, S5_NSTATE))
    if prompt:
        proj3 = proj.reshape(B, L, PROJ_W)
        y_s5, h_re, h_im = _s5(proj3, *h0, lp["s5"], steps=CHUNK)
        y_gdn, cbuf_n, gS_n = _gdn(proj3, cbuf, gS, lp["gdn"], c=CHUNK, nc=MIXER_CHUNKS_PER_STEP)
        m_in = jnp.broadcast_to(jnp.pad(mm, ((0, 0), (0, 8 - ML_HEADS)))[:, :, None], (B, 8, 128))
        y_ml, mC_n, mn_n, mm_n = _mlstm(proj3, mC.reshape(B, ML_QK_W, ML_DV), mn.reshape(B, 1, ML_QK_W), m_in,
                                        lp["ml"], c=CHUNK, nc=MIXER_CHUNKS_PER_STEP)
        mm_n = mm_n[:, :ML_HEADS, 0]
    else:
        y_s5, h_re, h_im = _s5(proj, *h0, lp["s5"], steps=1)
        y_gdn, cbuf_n, gS_n = _gdn_step(proj, cbuf, gS, lp["gdn"], nb=STEP_ROWS)
        y_ml, mC_n, mn_n, mm_n = _mlstm_step(proj, mC.reshape(B, ML_QK_W, ML_DV), mn.reshape(B, ML_QK_W),
                                             jnp.pad(mm, ((0, 0), (0, 128 - ML_HEADS))), lp["ml"], nb=STEP_ROWS)
        mm_n = mm_n[:, :ML_HEADS]
    x2d = _merge(x2d, lp["norm1_g"], lp["w_gate"], y_s5.reshape(B * L, 512), y_gdn.reshape(B * L, 512),
                 y_ml.reshape(B * L, 512), lp["w_br_s5"], lp["w_br_gdn"], lp["w_br_ml"], lp["w_out"], tm=tmm)
    x2d = _mlp(x2d, lp["norm2_g"], lp["w_up"], lp["w_down"], p2d, lp["w_ple"], lp["w_ple_gate"], gf,
               tm=tml, tf=tf, final=final)
    new_state = (h_re.reshape(B, S5_GROUPS, S5_STATE), h_im.reshape(B, S5_GROUPS, S5_STATE), cbuf_n, gS_n,
                 mC_n.reshape(B, ML_HEADS, ML_DK, ML_DV), mn_n.reshape(B, ML_HEADS, ML_DK), mm_n)
    return x2d, new_state


def kernel(x_prompt, x_sample, state_s5_re, state_s5_im, state_gdn_conv, state_gdn, state_mlstm_C, state_mlstm_n, state_mlstm_m, p_prompt, p_sample, norm1_g, w_in, s5_A_re, s5_A_im, s5_log_dt, s5_B_re, s5_B_im, s5_C_re, s5_C_im, s5_D, s5_w_glu, s5_b_glu, gdn_conv_w, gdn_A_log, gdn_dt_bias, gdn_norm_g, ml_b_i, ml_b_f, ml_norm_g, w_br_s5, w_br_gdn, w_br_ml, w_out, norm2_g, w_up, w_down, w_ple, w_ple_gate, final_norm_g):
    w = dict(norm1_g=norm1_g, w_in=w_in, s5_A_re=s5_A_re, s5_A_im=s5_A_im, s5_log_dt=s5_log_dt, s5_B_re=s5_B_re,
             s5_B_im=s5_B_im, s5_C_re=s5_C_re, s5_C_im=s5_C_im, s5_D=s5_D, s5_w_glu=s5_w_glu, s5_b_glu=s5_b_glu,
             gdn_conv_w=gdn_conv_w, gdn_A_log=gdn_A_log, gdn_dt_bias=gdn_dt_bias, gdn_norm_g=gdn_norm_g,
             ml_b_i=ml_b_i, ml_b_f=ml_b_f, ml_norm_g=ml_norm_g, w_br_s5=w_br_s5, w_br_gdn=w_br_gdn,
             w_br_ml=w_br_ml, w_out=w_out, norm2_g=norm2_g, w_up=w_up, w_down=w_down, w_ple=w_ple,
             w_ple_gate=w_ple_gate)
    bp, lp_len, _ = x_prompt.shape
    bs, ls_len, _ = x_sample.shape
    gf = final_norm_g.reshape(1, D_MODEL)
    xp = x_prompt.reshape(bp * lp_len, D_MODEL)
    xs = x_sample.reshape(bs * ls_len, D_MODEL)
    new_p, new_s = [], []
    for i in range(DEPTH):
        lp = _layer_params(i, w)
        zero_state = (jnp.zeros((bp, S5_GROUPS, S5_STATE), F32), jnp.zeros((bp, S5_GROUPS, S5_STATE), F32),
                      jnp.zeros((bp, GDN_CONV_W - 1, GDN_CONV_CH), F32),
                      jnp.zeros((bp, GDN_HEADS, GDN_DK, GDN_DV), F32),
                      jnp.zeros((bp, ML_HEADS, ML_DK, ML_DV), F32), jnp.zeros((bp, ML_HEADS, ML_DK), F32),
                      jnp.zeros((bp, ML_HEADS), F32))
        final = i == DEPTH - 1
        xp, sp_i = _block(xp, p_prompt[i].reshape(bp * lp_len, PLE_DIM), bp, lp_len, zero_state, lp, gf, final)
        sample_state = (state_s5_re[i], state_s5_im[i], state_gdn_conv[i], state_gdn[i], state_mlstm_C[i],
                        state_mlstm_n[i], state_mlstm_m[i])
        xs, ss_i = _block(xs, p_sample[i].reshape(bs * ls_len, PLE_DIM), bs, ls_len, sample_state, lp, gf, final)
        new_p.append(sp_i)
        new_s.append(ss_i)
    sp = [jnp.stack(t) for t in zip(*new_p)]
    ss = [jnp.stack(t) for t in zip(*new_s)]
    return (xp.reshape(bp, lp_len, D_MODEL), xs.reshape(bs, ls_len, D_MODEL),
            sp[0], ss[0], sp[1], ss[1], sp[2], ss[2], sp[3], ss[3], sp[4], ss[4], sp[5], ss[5], sp[6], ss[6])
```
